```python
import functools
import math
import jax
import jax.numpy as jnp
from jax import lax
import numpy as np

D_MODEL = 1024
BATCH = 4
SEQ = 4096
DEPTH = 4
DEC_BATCH = 128
DEC_SEQ = 4
PAST_LEN = 2048
PAGE_SIZE = 128

ATT_HEADS = 8
HEAD_DIM = 64
ATT_WIDTH = ATT_HEADS * HEAD_DIM
Q_BLOCK = 128
NEG_INF = -1e30
S5_WIDTH = D_MODEL // 4
S5_GROUP = 16
S5_GROUPS = S5_WIDTH // S5_GROUP
S5_STATE = 64
DT_MIN = 0.001
DT_MAX = 0.1
POOL_WIDTH = D_MODEL // 4
POOL_WINDOWS = (2, 4, 8, 16)
POOL_GROUPS = len(POOL_WINDOWS)
POOL_GW = POOL_WIDTH // POOL_GROUPS
POOL_BUF = max(POOL_WINDOWS) - 1
N_BRANCH = 3
IN_SPLITS = (ATT_WIDTH, 2 * ATT_WIDTH, 3 * ATT_WIDTH, 3 * ATT_WIDTH + ATT_HEADS,
             3 * ATT_WIDTH + ATT_HEADS + S5_WIDTH,
             3 * ATT_WIDTH + ATT_HEADS + S5_WIDTH + POOL_WIDTH)
IN_COLS = IN_SPLITS[-1] + N_BRANCH * D_MODEL
D_FF = ((8 * D_MODEL // 3 + 255) // 256) * 256
CONV_W = 3
RMS_EPS = 1e-6

kernel_name = 'fox_s5_pool_gated_hybrid_step'

F32 = jnp.float32


def rms_norm(x, g):
    xf = x.astype(F32)
    y = xf * lax.rsqrt(jnp.mean(xf * xf, axis=-1, keepdims=True) + RMS_EPS)
    return (y * g.astype(F32)).astype(x.dtype)


def fox_attend(q, k, v, cq, ck, qpos, kpos):
    s = jnp.einsum('bqhd,bkhd->bhqk', q, k, preferred_element_type=F32) * (HEAD_DIM ** -0.5)
    s = s + cq[..., :, None] - ck[..., None, :]
    s = jnp.where(kpos[None, :] <= qpos[:, None], s, NEG_INF)
    p = jax.nn.softmax(s, axis=-1)
    return jnp.einsum('bhqk,bkhd->bqhd', p.astype(v.dtype), v)


def fox_prompt(q, k, v, logf):
    bn, L = q.shape[:2]
    c = jnp.swapaxes(jnp.cumsum(logf.astype(F32), axis=1), 1, 2)
    nb = L // Q_BLOCK
    pos = jnp.arange(L)
    qb = jnp.swapaxes(q.reshape(bn, nb, Q_BLOCK, ATT_HEADS, HEAD_DIM), 0, 1)
    cb = jnp.moveaxis(c.reshape(bn, ATT_HEADS, nb, Q_BLOCK), 2, 0)
    pb = pos.reshape(nb, Q_BLOCK)
    out = lax.map(lambda a: fox_attend(a[0], k, v, a[1], c, a[2], pos), (qb, cb, pb))
    return jnp.swapaxes(out, 0, 1).reshape(bn, L, ATT_HEADS, HEAD_DIM)


def fox_sample(q, k, v, logf, k_pages, v_pages, lf_pages, page_table):
    bd, T = q.shape[:2]
    k_past = k_pages[page_table].reshape(bd, -1, ATT_HEADS, HEAD_DIM)
    v_past = v_pages[page_table].reshape(bd, -1, ATT_HEADS, HEAD_DIM)
    lf_past = lf_pages[page_table].reshape(bd, -1, ATT_HEADS)
    past = k_past.shape[1]
    k_all = jnp.concatenate([k_past.astype(k.dtype), k], axis=1)
    v_all = jnp.concatenate([v_past.astype(v.dtype), v], axis=1)
    lf_all = jnp.concatenate([lf_past.astype(F32), logf.astype(F32)], axis=1)
    c = jnp.swapaxes(jnp.cumsum(lf_all, axis=1), 1, 2)
    kpos = jnp.arange(past + T)
    qpos = past + jnp.arange(T)
    return fox_attend(q, k_all, v_all, c[:, :, past:], c, qpos, kpos)


def s5_branch(u, h0_re, h0_im, lp):
    bn, L, _ = u.shape
    uf = u.astype(F32).reshape(bn, L, S5_GROUPS, S5_GROUP)
    dt = jnp.exp(lp['s5_log_dt'].astype(F32))[:, None]
    lr = lp['s5_lam_re'].astype(F32)
    li = lp['s5_lam_im'].astype(F32)
    mag = jnp.exp(lr * dt)
    a_re = mag * jnp.cos(li * dt)
    a_im = mag * jnp.sin(li * dt)
    den = lr * lr + li * li
    z_re = ((a_re - 1.0) * lr + a_im * li) / den
    z_im = (a_im * lr - (a_re - 1.0) * li) / den
    bu_re = jnp.einsum('blgm,gpm->blgp', uf, lp['s5_b_re'].astype(F32))
    bu_im = jnp.einsum('blgm,gpm->blgp', uf, lp['s5_b_im'].astype(F32))
    x_re = z_re * bu_re - z_im * bu_im
    x_im = z_re * bu_im + z_im * bu_re
    h0r = h0_re.astype(F32)
    h0i = h0_im.astype(F32)
    x_re = x_re.at[:, 0].add(a_re * h0r - a_im * h0i)
    x_im = x_im.at[:, 0].add(a_re * h0i + a_im * h0r)
    A_re = jnp.broadcast_to(a_re, x_re.shape)
    A_im = jnp.broadcast_to(a_im, x_im.shape)

    def combine(e1, e2):
        a1r, a1i, b1r, b1i = e1
        a2r, a2i, b2r, b2i = e2
        return (a1r * a2r - a1i * a2i, a1r * a2i + a1i * a2r,
                a2r * b1r - a2i * b1i + b2r, a2r * b1i + a2i * b1r + b2i)

    _, _, h_re, h_im = lax.associative_scan(combine, (A_re, A_im, x_re, x_im), axis=1)
    y = (jnp.einsum('gmp,blgp->blgm', lp['s5_c_re'].astype(F32), h_re)
         - jnp.einsum('gmp,blgp->blgm', lp['s5_c_im'].astype(F32), h_im))
    y = y.reshape(bn, L, S5_WIDTH) + lp['s5_d'].astype(F32) * u.astype(F32)
    z = jax.nn.gelu(y).astype(u.dtype)
    out = (z @ lp['s5_glu_w1'] + lp['s5_glu_b1']) * jax.nn.sigmoid(z @ lp['s5_glu_w2'] + lp['s5_glu_b2'])
    return out, h_re[:, -1], h_im[:, -1]


def pool_branch(p, buf, start_pos, lp):
    bn, L, _ = p.shape
    xc = jnp.concatenate([buf.astype(p.dtype), p], axis=1)
    cs = jnp.cumsum(xc.astype(F32), axis=1)
    cs = jnp.concatenate([jnp.zeros_like(cs[:, :1]), cs], axis=1)
    pos = start_pos + jnp.arange(L)
    outs = []
    for g, w in enumerate(POOL_WINDOWS):
        sl = slice(g * POOL_GW, (g + 1) * POOL_GW)
        wsum = cs[:, POOL_BUF + 1:POOL_BUF + 1 + L, sl] - cs[:, POOL_BUF + 1 - w:POOL_BUF + 1 - w + L, sl]
        cnt = jnp.minimum(w, pos + 1).astype(F32)[None, :, None]
        outs.append(wsum / cnt - p[..., sl].astype(F32))
    m = jnp.stack(outs, axis=2)
    y = jnp.einsum('blgc,gcd->blgd', m, lp['pool_w'].astype(F32)).reshape(bn, L, POOL_WIDTH)
    y = y * lp['pool_scale'].astype(F32)
    return y.astype(p.dtype), xc[:, -POOL_BUF:]


def conv_ffn(h, buf, lp):
    u = h @ lp['w_up']
    L = u.shape[1]
    uc = jnp.concatenate([buf.astype(u.dtype), u], axis=1)
    cw = lp['conv_w']
    cv = lp['conv_b'] + cw[0] * uc[:, 0:L] + cw[1] * uc[:, 1:L + 1] + cw[2] * uc[:, 2:L + 2]
    gate, val = jnp.split(cv, 2, axis=-1)
    return (jax.nn.gelu(gate) * val) @ lp['w_down'], uc[:, -(CONV_W - 1):]


def layer_forward(x, lp, attn, h0_re, h0_im, pool_buf, conv_buf, start_pos):
    bn, L = x.shape[:2]
    h = rms_norm(x, lp['norm_mix'])
    proj = h @ lp['w_in']
    q, k, v, fpre, u_s5, u_pool, gates = jnp.split(proj, IN_SPLITS, axis=-1)
    q = q.reshape(bn, L, ATT_HEADS, HEAD_DIM)
    k = k.reshape(bn, L, ATT_HEADS, HEAD_DIM)
    v = v.reshape(bn, L, ATT_HEADS, HEAD_DIM)
    logf = jax.nn.log_sigmoid(fpre.astype(F32) + lp['b_f'].astype(F32))
    ya = attn(q, k, v, logf).reshape(bn, L, ATT_WIDTH)
    ys, hr, hi = s5_branch(u_s5, h0_re, h0_im, lp)
    yp, new_pool = pool_branch(u_pool, pool_buf, start_pos, lp)
    g = jax.nn.sigmoid(gates.astype(F32)).reshape(bn, L, N_BRANCH, D_MODEL)
    merged = (g[:, :, 0] * (ya @ lp['w_br_a']) + g[:, :, 1] * (ys @ lp['w_br_b'])
              + g[:, :, 2] * (yp @ lp['w_br_c']))
    x = x + merged.astype(x.dtype) @ lp['w_out']
    f_out, new_conv = conv_ffn(rms_norm(x, lp['norm_ffn']), conv_buf, lp)
    x = x + f_out
    return x, (k, v, logf, hr, hi, new_pool, new_conv)


def setup_inputs(seed: int = 0) -> dict:
    key = jax.random.key(seed)
    ks = list(jax.random.split(key, 48))

    def nrm(shape, scale):
        return jax.random.normal(ks.pop(), shape, F32) * scale

    n_pages = PAST_LEN // PAGE_SIZE
    n_used = DEC_BATCH * n_pages
    n_pool = (n_used * 5) // 4
    perm = jax.random.permutation(ks.pop(), n_pool)
    page_table = perm[:n_used].reshape(DEC_BATCH, n_pages).astype(jnp.int32)

    inp = {}
    inp['x_prompt'] = nrm((BATCH, SEQ, D_MODEL), 1.0)
    inp['x_sample'] = nrm((DEC_BATCH, DEC_SEQ, D_MODEL), 1.0)
    inp['cache_k'] = nrm((DEPTH, n_pool, PAGE_SIZE, ATT_HEADS, HEAD_DIM), 1.0)
    inp['cache_v'] = nrm((DEPTH, n_pool, PAGE_SIZE, ATT_HEADS, HEAD_DIM), 1.0)
    inp['cache_logf'] = jax.nn.log_sigmoid(3.0 + nrm((DEPTH, n_pool, PAGE_SIZE, ATT_HEADS), 1.0))
    inp['state_ssm_re'] = nrm((DEPTH, DEC_BATCH, S5_GROUPS, S5_STATE), 0.3)
    inp['state_ssm_im'] = nrm((DEPTH, DEC_BATCH, S5_GROUPS, S5_STATE), 0.3)
    inp['state_pool'] = nrm((DEPTH, DEC_BATCH, POOL_BUF, POOL_WIDTH), 1.0)
    inp['state_ffn_conv'] = nrm((DEPTH, DEC_BATCH, CONV_W - 1, 2 * D_FF), 1.0)
    inp['page_table'] = page_table
    inp['norm_mix_g'] = 1.0 + nrm((DEPTH, D_MODEL), 0.01)
    inp['w_in'] = nrm((DEPTH, D_MODEL, IN_COLS), D_MODEL ** -0.5)
    inp['b_f'] = 3.0 + nrm((DEPTH, ATT_HEADS), 0.1)
    inp['s5_lam_re'] = -0.5 + nrm((DEPTH, S5_GROUPS, S5_STATE), 0.01)
    inp['s5_lam_im'] = jnp.pi * jnp.arange(S5_STATE, dtype=F32) + nrm((DEPTH, S5_GROUPS, S5_STATE), 0.01)
    inp['s5_log_dt'] = jax.random.uniform(ks.pop(), (DEPTH, S5_GROUPS), F32,
                                          minval=math.log(DT_MIN), maxval=math.log(DT_MAX))
    inp['s5_b_re'] = nrm((DEPTH, S5_GROUPS, S5_STATE, S5_GROUP), (2 * S5_GROUP) ** -0.5)
    inp['s5_b_im'] = nrm((DEPTH, S5_GROUPS, S5_STATE, S5_GROUP), (2 * S5_GROUP) ** -0.5)
    inp['s5_c_re'] = nrm((DEPTH, S5_GROUPS, S5_GROUP, S5_STATE), (2 * S5_STATE) ** -0.5)
    inp['s5_c_im'] = nrm((DEPTH, S5_GROUPS, S5_GROUP, S5_STATE), (2 * S5_STATE) ** -0.5)
    inp['s5_d'] = nrm((DEPTH, S5_WIDTH), 1.0)
    inp['s5_glu_w1'] = nrm((DEPTH, S5_WIDTH, S5_WIDTH), S5_WIDTH ** -0.5)
    inp['s5_glu_b1'] = nrm((DEPTH, S5_WIDTH), 0.01)
    inp['s5_glu_w2'] = nrm((DEPTH, S5_WIDTH, S5_WIDTH), S5_WIDTH ** -0.5)
    inp['s5_glu_b2'] = nrm((DEPTH, S5_WIDTH), 0.01)
    inp['pool_w'] = nrm((DEPTH, POOL_GROUPS, POOL_GW, POOL_GW), POOL_GW ** -0.5)
    inp['pool_scale'] = 1.0 + nrm((DEPTH, POOL_WIDTH), 0.1)
    inp['w_br_a'] = nrm((DEPTH, ATT_WIDTH, D_MODEL), ATT_WIDTH ** -0.5)
    inp['w_br_b'] = nrm((DEPTH, S5_WIDTH, D_MODEL), S5_WIDTH ** -0.5)
    inp['w_br_c'] = nrm((DEPTH, POOL_WIDTH, D_MODEL), POOL_WIDTH ** -0.5)
    inp['w_out'] = nrm((DEPTH, D_MODEL, D_MODEL), D_MODEL ** -0.5)
    inp['norm_ffn_g'] = 1.0 + nrm((DEPTH, D_MODEL), 0.01)
    inp['w_up'] = nrm((DEPTH, D_MODEL, 2 * D_FF), D_MODEL ** -0.5)
    inp['conv_w'] = nrm((DEPTH, CONV_W, 2 * D_FF), CONV_W ** -0.5)
    inp['conv_b'] = nrm((DEPTH, 2 * D_FF), 0.01)
    inp['w_down'] = nrm((DEPTH, D_FF, D_MODEL), D_FF ** -0.5)
    inp['norm_final_g'] = 1.0 + nrm((D_MODEL,), 0.01)
    return inp


def reference(x_prompt, x_sample, cache_k, cache_v, cache_logf, state_ssm_re, state_ssm_im,
              state_pool, state_ffn_conv, page_table,
              norm_mix_g, w_in, b_f, s5_lam_re, s5_lam_im, s5_log_dt, s5_b_re, s5_b_im,
              s5_c_re, s5_c_im, s5_d, s5_glu_w1, s5_glu_b1, s5_glu_w2, s5_glu_b2,
              pool_w, pool_scale, w_br_a, w_br_b, w_br_c, w_out,
              norm_ffn_g, w_up, conv_w, conv_b, w_down, norm_final_g):
    bp = x_prompt.shape[0]
    past_len = page_table.shape[1] * cache_k.shape[2]
    h0 = jnp.zeros((bp, S5_GROUPS, S5_STATE), F32)
    pool0 = jnp.zeros((bp, POOL_BUF, POOL_WIDTH), x_prompt.dtype)
    conv0 = jnp.zeros((bp, CONV_W - 1, 2 * D_FF), x_prompt.dtype)
    xp, xs = x_prompt, x_sample
    st_p, st_s = [], []
    for l in range(DEPTH):
        lp = dict(norm_mix=norm_mix_g[l], w_in=w_in[l], b_f=b_f[l],
                  s5_lam_re=s5_lam_re[l], s5_lam_im=s5_lam_im[l], s5_log_dt=s5_log_dt[l],
                  s5_b_re=s5_b_re[l], s5_b_im=s5_b_im[l], s5_c_re=s5_c_re[l], s5_c_im=s5_c_im[l],
                  s5_d=s5_d[l], s5_glu_w1=s5_glu_w1[l], s5_glu_b1=s5_glu_b1[l],
                  s5_glu_w2=s5_glu_w2[l], s5_glu_b2=s5_glu_b2[l],
                  pool_w=pool_w[l], pool_scale=pool_scale[l],
                  w_br_a=w_br_a[l], w_br_b=w_br_b[l], w_br_c=w_br_c[l], w_out=w_out[l],
                  norm_ffn=norm_ffn_g[l], w_up=w_up[l], conv_w=conv_w[l], conv_b=conv_b[l],
                  w_down=w_down[l])
        xp, sp = layer_forward(xp, lp, fox_prompt, h0, h0, pool0, conv0, 0)
        attn_s = functools.partial(fox_sample, k_pages=cache_k[l], v_pages=cache_v[l],
                                   lf_pages=cache_logf[l], page_table=page_table)
        xs, ss = layer_forward(xs, lp, attn_s, state_ssm_re[l], state_ssm_im[l],
                               state_pool[l], state_ffn_conv[l], past_len)
        st_p.append(sp)
        st_s.append(ss)
    y_prompt = rms_norm(xp, norm_final_g)
    y_sample = rms_norm(xs, norm_final_g)
    k_p, v_p, lf_p, hr_p, hi_p, pool_p, conv_p = [jnp.stack(z) for z in zip(*st_p)]
    k_s, v_s, lf_s, hr_s, hi_s, pool_s, conv_s = [jnp.stack(z) for z in zip(*st_s)]
    return (y_prompt, y_sample, k_p, v_p, lf_p, hr_p, hi_p, pool_p, conv_p,
            k_s, v_s, lf_s, hr_s, hi_s, pool_s, conv_s)
```

```python
import functools

import jax
import jax.numpy as jnp
from jax import lax
from jax.experimental import pallas as pl
from jax.experimental.pallas import tpu as pltpu

F32 = jnp.float32
BF16 = jnp.bfloat16

ATT_HEADS = 8
HEAD_DIM = 64
ATT_WIDTH = ATT_HEADS * HEAD_DIM
S5_WIDTH = 256
S5_GROUPS = 16
S5_STATE = 64
S5_N = S5_GROUPS * S5_STATE
POOL_WIDTH = 256
POOL_WINDOWS = (2, 4, 8, 16)
POOL_GW = 64
POOL_BUF = 15
POOL_HALO = 16
CONV_TAPS = 3
RMS_EPS = 1e-6
NEG_INF = -1e30
LANES = 128
SUBLANES = 8
VMEM_LIMIT = 56 * 1024 * 1024
HIGHEST = lax.Precision.HIGHEST


def _params(*sem):
    return pltpu.CompilerParams(dimension_semantics=sem, vmem_limit_bytes=VMEM_LIMIT)


def _resident(shape):
    nd = len(shape)
    return pl.BlockSpec(shape, lambda *_: (0,) * nd, pipeline_mode=pl.Buffered(1))


def _rms(x, g):
    return x * lax.rsqrt(jnp.mean(x * x, axis=-1, keepdims=True) + RMS_EPS) * g


def _sigmoid(x):
    return 1.0 / (1.0 + jnp.exp(-x))


def _gelu(x):
    return 0.5 * x * (1.0 + jnp.tanh(0.7978845608028654 * (x + 0.044715 * (x * x * x))))


def _log2(n):
    assert n & (n - 1) == 0, n
    return n.bit_length() - 1


def _dot(a, b):
    return jnp.dot(a, b, preferred_element_type=F32)


def _dot_nt(a, b):
    return lax.dot_general(a, b, (((1,), (1,)), ((), ())), preferred_element_type=F32)


def _proj_kernel(x_ref, g_ref, wqkv_ref, wf_ref, wsp_ref, bf_ref,
                 q_ref, k_ref, v_ref, kb_ref, vb_ref, lf_ref, c_ref, us_ref, up_ref,
                 carry_ref, *, tm):
    i = pl.program_id(1)
    h = _rms(x_ref[...], g_ref[...]).astype(BF16)
    qkv = _dot(h, wqkv_ref[...])
    q_ref[...] = (qkv[:, :ATT_WIDTH] * (HEAD_DIM ** -0.5)).astype(BF16)
    k = qkv[:, ATT_WIDTH:2 * ATT_WIDTH]
    v = qkv[:, 2 * ATT_WIDTH:]
    k_ref[...] = k
    v_ref[...] = v
    kb_ref[...] = k.astype(BF16)
    vb_ref[...] = v.astype(BF16)
    f = _dot(h, wf_ref[...]) + bf_ref[...]
    lf = -(jnp.maximum(-f, 0.0) + jnp.log1p(jnp.exp(-jnp.abs(f))))
    lf_ref[...] = lf[:, :ATT_HEADS]

    @pl.when(i == 0)
    def _():
        carry_ref[...] = jnp.zeros_like(carry_ref)

    row = lax.broadcasted_iota(jnp.int32, (tm, tm), 0)
    col = lax.broadcasted_iota(jnp.int32, (tm, tm), 1)
    tri = (row >= col).astype(F32)
    c = carry_ref[...] + jnp.dot(tri, lf, precision=HIGHEST, preferred_element_type=F32)
    c_ref[...] = c[:, :ATT_HEADS]
    carry_ref[...] = c[tm - 1:tm, :]
    sp = _dot(h, wsp_ref[...])
    us_ref[...] = sp[:, :S5_WIDTH]
    up_ref[...] = sp[:, S5_WIDTH:]


def _proj(x, g, wqkv, wf, wsp, bf, tm):
    B, L, D = x.shape
    nt = L // tm
    row = lambda w: pl.BlockSpec((None, tm, w), lambda b, i: (b, i, 0))
    out_shape = (
        jax.ShapeDtypeStruct((B, L, ATT_WIDTH), BF16),
        jax.ShapeDtypeStruct((B, L, ATT_WIDTH), F32),
        jax.ShapeDtypeStruct((B, L, ATT_WIDTH), F32),
        jax.ShapeDtypeStruct((B, L, ATT_WIDTH), BF16),
        jax.ShapeDtypeStruct((B, L, ATT_WIDTH), BF16),
        jax.ShapeDtypeStruct((B, L, ATT_HEADS), F32),
        jax.ShapeDtypeStruct((B, L, ATT_HEADS), F32),
        jax.ShapeDtypeStruct((B, L, S5_WIDTH), F32),
        jax.ShapeDtypeStruct((B, L, POOL_WIDTH), F32),
    )
    return pl.pallas_call(
        functools.partial(_proj_kernel, tm=tm),
        grid=(B, nt),
        in_specs=[row(D), _resident(g.shape), _resident(wqkv.shape), _resident(wf.shape),
                  _resident(wsp.shape), _resident(bf.shape)],
        out_specs=(row(ATT_WIDTH),) * 5 + (row(ATT_HEADS),) * 2 + (row(S5_WIDTH), row(POOL_WIDTH)),
        out_shape=out_shape,
        scratch_shapes=[pltpu.VMEM((1, LANES), F32)],
        compiler_params=_params("arbitrary", "arbitrary"),
        name="proj",
    )(x, g, wqkv, wf, wsp, bf)


def _attn_kernel(q_ref, k_ref, v_ref, c_ref, o_ref, m_ref, l_ref, acc_ref, *, tq):
    qi = pl.program_id(2)
    q2 = q_ref[...]
    lane = lax.broadcasted_iota(jnp.int32, (tq, LANES), 1)
    lo = lane < HEAD_DIM
    zero = jnp.zeros_like(q2)
    qs = (jnp.where(lo, q2, zero), jnp.where(lo, zero, q2))
    m_ref[...] = jnp.full_like(m_ref, NEG_INF)
    l_ref[...] = jnp.zeros_like(l_ref)
    acc_ref[...] = jnp.zeros_like(acc_ref)
    row = lax.broadcasted_iota(jnp.int32, (tq, tq), 0)
    col = lax.broadcasted_iota(jnp.int32, (tq, tq), 1)
    causal = col <= row

    def block(j, masked):
        start = pl.multiple_of(j * tq, tq)
        kj = k_ref[pl.ds(start, tq), :]
        vj = v_ref[pl.ds(start, tq), :]
        cj = c_ref[:, pl.ds(start, tq)]
        for e in range(2):
            s = _dot_nt(qs[e], kj) - cj[e:e + 1, :]
            if masked:
                s = jnp.where(causal, s, NEG_INF)
            m_old = m_ref[e]
            m_new = jnp.maximum(m_old, jnp.max(s, axis=-1, keepdims=True))
            alpha = jnp.exp(m_old - m_new)
            p = jnp.exp(s - m_new)
            l_ref[e] = alpha * l_ref[e] + jnp.sum(p, axis=-1, keepdims=True)
            acc_ref[e] = alpha * acc_ref[e] + _dot(p.astype(BF16), vj)
            m_ref[e] = m_new

    def body(j, carry):
        block(j, False)
        return carry

    lax.fori_loop(0, qi, body, 0)
    block(qi, True)
    out = jnp.where(lo, acc_ref[0] / l_ref[0], acc_ref[1] / l_ref[1])
    o_ref[...] = out.astype(BF16)


def _attn(q, kb, vb, ct, tq):
    B, L, _ = q.shape
    nq = L // tq
    npair = ATT_HEADS // 2
    return pl.pallas_call(
        functools.partial(_attn_kernel, tq=tq),
        grid=(B, npair, nq),
        in_specs=[
            pl.BlockSpec((None, tq, LANES), lambda b, hp, qi: (b, qi, hp)),
            pl.BlockSpec((None, L, LANES), lambda b, hp, qi: (b, 0, hp)),
            pl.BlockSpec((None, L, LANES), lambda b, hp, qi: (b, 0, hp)),
            pl.BlockSpec((None, None, 2, L), lambda b, hp, qi: (b, hp, 0, 0)),
        ],
        out_specs=pl.BlockSpec((None, tq, LANES), lambda b, hp, qi: (b, qi, hp)),
        out_shape=jax.ShapeDtypeStruct((B, L, ATT_WIDTH), BF16),
        scratch_shapes=[pltpu.VMEM((2, tq, 1), F32), pltpu.VMEM((2, tq, 1), F32),
                        pltpu.VMEM((2, tq, LANES), F32)],
        compiler_params=_params("arbitrary", "arbitrary", "arbitrary"),
        name="attn_prompt",
    )(q, kb, vb, ct)


def _attn_sample_kernel(pt_ref, q_ref, kn_ref, vn_ref, lfn_ref, *rest, n_pages, page, t_new):
    del pt_ref
    k_refs = rest[:n_pages]
    v_refs = rest[n_pages:2 * n_pages]
    lf_refs = rest[2 * n_pages:3 * n_pages]
    o_ref = rest[3 * n_pages]
    nrow = t_new * ATT_HEADS
    sub = lax.broadcasted_iota(jnp.int32, (ATT_HEADS, ATT_WIDTH), 0)
    lane = lax.broadcasted_iota(jnp.int32, (ATT_HEADS, ATT_WIDTH), 1)
    head_mask = jnp.right_shift(lane, _log2(HEAD_DIM)) == sub
    q = q_ref[...].astype(F32)
    qexp = jnp.concatenate(
        [jnp.where(head_mask, jnp.broadcast_to(q[t:t + 1, :], (ATT_HEADS, ATT_WIDTH)), 0.0)
         for t in range(t_new)], axis=0)
    qexp_b = qexp.astype(BF16)
    r = lax.broadcasted_iota(jnp.int32, (page, page), 0)
    c = lax.broadcasted_iota(jnp.int32, (page, page), 1)
    upper = (r <= c).astype(F32)

    carry = jnp.zeros((ATT_HEADS, 1), F32)
    s_pages = []
    for j in range(n_pages):
        kj = k_refs[j][...].astype(BF16)
        cj = carry + jnp.dot(lf_refs[j][...], upper, precision=HIGHEST, preferred_element_type=F32)
        carry = cj[:, page - 1:page]
        s_pages.append(_dot_nt(qexp_b, kj) - jnp.concatenate([cj] * t_new, axis=0))

    kn = kn_ref[...].astype(BF16).astype(F32)
    vn = vn_ref[...].astype(BF16).astype(F32)
    lfn = lfn_ref[...]
    tq_of_row = jnp.right_shift(lax.broadcasted_iota(jnp.int32, (nrow, 1), 0), _log2(ATT_HEADS))
    s_new = []
    for t in range(t_new):
        carry = carry + lfn[:, t:t + 1]
        st = jnp.sum(qexp * kn[t:t + 1, :], axis=-1, keepdims=True) - jnp.concatenate([carry] * t_new, axis=0)
        s_new.append(jnp.where(tq_of_row >= t, st, NEG_INF))

    m = s_new[0]
    for st in s_new[1:]:
        m = jnp.maximum(m, st)
    for s in s_pages:
        m = jnp.maximum(m, jnp.max(s, axis=-1, keepdims=True))
    l = jnp.zeros((nrow, 1), F32)
    acc = jnp.zeros((nrow, ATT_WIDTH), F32)
    for j in range(n_pages):
        p = jnp.exp(s_pages[j] - m)
        l = l + jnp.sum(p, axis=-1, keepdims=True)
        acc = acc + _dot(p.astype(BF16), v_refs[j][...].astype(BF16))
    for t in range(t_new):
        p = jnp.exp(s_new[t] - m)
        l = l + p
        acc = acc + p.astype(BF16).astype(F32) * vn[t:t + 1, :]
    out = acc / l
    rows = [jnp.sum(jnp.where(head_mask, out[t * ATT_HEADS:(t + 1) * ATT_HEADS, :], 0.0), axis=0, keepdims=True)
            for t in range(t_new)]
    o_ref[...] = jnp.concatenate(rows, axis=0).astype(BF16)


def _attn_sample(page_table, q, kn, vn, lfn, cache_k, cache_v, cache_lft, layer):
    Bd, T, _ = q.shape
    n_pages = page_table.shape[1]
    page = cache_k.shape[2]

    def page_spec(shape, j):
        return pl.BlockSpec((None, None) + shape, lambda b, pt: (layer, pt[b, j], 0, 0))

    seq = lambda shape: pl.BlockSpec((None,) + shape, lambda b, pt: (b, 0, 0))
    in_specs = [seq((T, ATT_WIDTH)), seq((T, ATT_WIDTH)), seq((T, ATT_WIDTH)), seq((ATT_HEADS, LANES))]
    in_specs += [page_spec((page, ATT_WIDTH), j) for j in range(n_pages)]
    in_specs += [page_spec((page, ATT_WIDTH), j) for j in range(n_pages)]
    in_specs += [page_spec((ATT_HEADS, page), j) for j in range(n_pages)]
    grid_spec = pltpu.PrefetchScalarGridSpec(
        num_scalar_prefetch=1, grid=(Bd,), in_specs=in_specs,
        out_specs=pl.BlockSpec((None, T, ATT_WIDTH), lambda b, pt: (b, 0, 0)))
    return pl.pallas_call(
        functools.partial(_attn_sample_kernel, n_pages=n_pages, page=page, t_new=T),
        grid_spec=grid_spec,
        out_shape=jax.ShapeDtypeStruct((Bd, T, ATT_WIDTH), BF16),
        compiler_params=_params("arbitrary"),
        name="attn_sample",
    )(page_table, q, kn, vn, lfn, *([cache_k] * n_pages), *([cache_v] * n_pages), *([cache_lft] * n_pages))


def _s5_kernel(u_ref, h0_ref, lre_ref, lim_ref, ldt_ref, bblk_ref, cblk_ref, d_ref,
               w1_ref, b1_ref, w2_ref, b2_ref, y_ref, hfin_ref,
               x_scr, h_scr, st_scr, end_scr, *, R, Tt, two_pass):
    p = pl.program_id(0)
    i = pl.program_id(1)
    nt = pl.num_programs(1)
    n = S5_N
    last_pass = 1 if two_pass else 0

    dt = jnp.exp(ldt_ref[...])
    lr = lre_ref[...]
    li = lim_ref[...]
    mag = jnp.exp(lr * dt)
    a_re = mag * jnp.cos(li * dt)
    a_im = mag * jnp.sin(li * dt)
    den = lr * lr + li * li
    z_re = ((a_re - 1.0) * lr + a_im * li) / den
    z_im = (a_im * lr - (a_re - 1.0) * li) / den

    u = u_ref[...]
    bu = _dot(u.astype(BF16), bblk_ref[...])
    bur = bu[:, :n]
    bui = bu[:, n:]
    x_scr[:, :n] = z_re * bur - z_im * bui
    x_scr[:, n:] = z_re * bui + z_im * bur

    @pl.when((i == 0) & (p == 0))
    def _():
        st_scr[...] = h0_ref[...]

    if two_pass:
        @pl.when((i == 0) & (p == 1))
        def _():
            rolled = pltpu.roll(end_scr[...], R // 2, axis=0)
            first = lax.broadcasted_iota(jnp.int32, (R, 2 * n), 0) < R // 2
            st_scr[...] = jnp.where(first, h0_ref[...], rolled)

    ar = jnp.broadcast_to(a_re, (R, n))
    ai = jnp.broadcast_to(a_im, (R, n))

    def scan(store):
        def step(t, carry):
            hr, hi = carry
            r0 = t * R if isinstance(t, int) else pl.multiple_of(t * R, R)
            xr = x_scr[pl.ds(r0, R), :n]
            xi = x_scr[pl.ds(r0, R), n:]
            nr = ar * hr - ai * hi + xr
            ni = ar * hi + ai * hr + xi
            if store:
                h_scr[pl.ds(r0, R), :n] = nr
                h_scr[pl.ds(r0, R), n:] = ni
            return nr, ni

        if R == SUBLANES:
            hr, hi = lax.fori_loop(0, Tt, step, (st_scr[:, :n], st_scr[:, n:]), unroll=2)
        else:
            hr, hi = st_scr[:, :n], st_scr[:, n:]
            for t in range(Tt):
                hr, hi = step(t, (hr, hi))
        st_scr[:, :n] = hr
        st_scr[:, n:] = hi

    if two_pass:
        @pl.when(p == 0)
        def _():
            scan(False)

            @pl.when(i == nt - 1)
            def _():
                end_scr[...] = st_scr[...]

    @pl.when(p == last_pass)
    def _():
        scan(True)
        y = _dot(h_scr[...].astype(BF16), cblk_ref[...]) + d_ref[...] * u
        z = _gelu(y).astype(BF16)
        out = (_dot(z, w1_ref[...]) + b1_ref[...]) * _sigmoid(_dot(z, w2_ref[...]) + b2_ref[...])
        y_ref[...] = out.astype(BF16)

        @pl.when(i == nt - 1)
        def _():
            hfin_ref[...] = st_scr[...]


def _s5(u, h0, lre, lim, ldt, bblk, cblk, d, w1, b1, w2, b2, *, R, Tt, two_pass):
    rows = u.shape[0]
    tile = R * Tt
    nt = rows // tile
    npass = 2 if two_pass else 1
    n2 = 2 * S5_N
    consts = (h0, lre, lim, ldt, bblk, cblk, d, w1, b1, w2, b2)
    return pl.pallas_call(
        functools.partial(_s5_kernel, R=R, Tt=Tt, two_pass=two_pass),
        grid=(npass, nt),
        in_specs=[pl.BlockSpec((tile, S5_WIDTH), lambda p, i: (i, 0))] + [_resident(a.shape) for a in consts],
        out_specs=(pl.BlockSpec((tile, S5_WIDTH), lambda p, i: (i * p if two_pass else i, 0)),
                   pl.BlockSpec((R, n2), lambda p, i: (0, 0))),
        out_shape=(jax.ShapeDtypeStruct((rows, S5_WIDTH), BF16), jax.ShapeDtypeStruct((R, n2), F32)),
        scratch_shapes=[pltpu.VMEM((tile, n2), F32), pltpu.VMEM((tile, n2), F32),
                        pltpu.VMEM((R, n2), F32), pltpu.VMEM((R, n2), F32)],
        compiler_params=_params("arbitrary", "arbitrary"),
        name="s5",
    )(u, *consts)


def _pool_kernel(p_ref, halo_ref, w_ref, sc_ref, y_ref, ext, a2, a4, a8, a16, *, R, tm, start_pos, nt):
    i = pl.program_id(1)
    H = POOL_HALO * R
    n = H + tm

    @pl.when(i == 0)
    def _():
        ext[0:H, :] = halo_ref[...]

    x = p_ref[...]
    ext[H:n, :] = x
    n2, n4, n8, n16 = n - R, n - 3 * R, n - 7 * R, n - 15 * R
    a2[0:n2, :] = ext[R:R + n2, :] + ext[0:n2, :]
    a4[0:n4, :] = a2[2 * R:2 * R + n4, :] + a2[0:n4, :]
    a8[0:n8, :] = a4[4 * R:4 * R + n8, :] + a4[0:n8, :]
    a16[0:n16, :] = a8[8 * R:8 * R + n16, :] + a8[0:n16, :]
    s2 = a2[15 * R:15 * R + tm, :]
    s4 = a4[13 * R:13 * R + tm, :]
    s8 = a8[9 * R:9 * R + tm, :]
    s16 = a16[R:R + tm, :]
    pos = start_pos + jnp.right_shift(i * tm + lax.broadcasted_iota(jnp.int32, (tm, POOL_WIDTH), 0), _log2(R))
    lane = lax.broadcasted_iota(jnp.int32, (tm, POOL_WIDTH), 1)
    cnt = lambda w: jnp.minimum(w, pos + 1).astype(F32)
    mean = jnp.where(lane < POOL_GW, s2 / cnt(2),
                     jnp.where(lane < 2 * POOL_GW, s4 / cnt(4),
                               jnp.where(lane < 3 * POOL_GW, s8 / cnt(8), s16 / cnt(16))))
    m = (mean - x).astype(BF16)
    y_ref[...] = (_dot(m, w_ref[...]) * sc_ref[...]).astype(BF16)
    if nt > 1:
        ext[0:H, :] = ext[tm:tm + H, :]


def _pool(p, halo, wblk, scale, *, R, tm, start_pos):
    G, L, W = p.shape
    nt = L // tm
    H = POOL_HALO * R
    buf = pltpu.VMEM((H + tm, W), F32)
    return pl.pallas_call(
        functools.partial(_pool_kernel, R=R, tm=tm, start_pos=start_pos, nt=nt),
        grid=(G, nt),
        in_specs=[pl.BlockSpec((None, tm, W), lambda g, i: (g, i, 0)),
                  pl.BlockSpec((None, H, W), lambda g, i: (g, 0, 0)),
                  _resident(wblk.shape), _resident(scale.shape)],
        out_specs=pl.BlockSpec((None, tm, W), lambda g, i: (g, i, 0)),
        out_shape=jax.ShapeDtypeStruct((G, L, W), BF16),
        scratch_shapes=[buf] * 5,
        compiler_params=_params("arbitrary", "arbitrary"),
        name="pool",
    )(p, halo, wblk, scale)


def _merge_kernel(x_ref, ya_ref, ys_ref, yp_ref, g1_ref, wg_ref, wa_ref, wb_ref, wc_ref, wo_ref, g2_ref,
                  x1_ref, h2_ref):
    x = x_ref[...]
    d = x.shape[-1]
    h = _rms(x, g1_ref[...]).astype(BF16)
    gates = _dot(h, wg_ref[...])
    merged = (_sigmoid(gates[:, :d]) * _dot(ya_ref[...], wa_ref[...])
              + _sigmoid(gates[:, d:2 * d]) * _dot(ys_ref[...], wb_ref[...])
              + _sigmoid(gates[:, 2 * d:]) * _dot(yp_ref[...], wc_ref[...]))
    x1 = x + _dot(merged.astype(BF16), wo_ref[...])
    x1_ref[...] = x1
    h2_ref[...] = _rms(x1, g2_ref[...]).astype(BF16)


def _merge(x, ya, ys, yp, g1, wg, wa, wb, wc, wo, g2, tm):
    N, D = x.shape
    row = lambda w: pl.BlockSpec((tm, w), lambda i: (i, 0))
    consts = (g1, wg, wa, wb, wc, wo, g2)
    return pl.pallas_call(
        _merge_kernel,
        grid=(N // tm,),
        in_specs=[row(D), row(ATT_WIDTH), row(S5_WIDTH), row(POOL_WIDTH)] + [_resident(a.shape) for a in consts],
        out_specs=(row(D), row(D)),
        out_shape=(jax.ShapeDtypeStruct((N, D), F32), jax.ShapeDtypeStruct((N, D), BF16)),
        compiler_params=_params("arbitrary"),
        name="merge",
    )(x, ya, ys, yp, *consts)


def _ffn_kernel(h2_ref, x1_ref, halo_ref, wup_ref, cw_ref, cb_ref, wdn_ref, gfin_ref,
                x2_ref, tail_ref, ext, tail, *, R, tm, H, fc, final_norm):
    i = pl.program_id(1)
    nt = pl.num_programs(1)
    dff = wdn_ref.shape[0]

    @pl.when(i == 0)
    def _():
        tail[...] = halo_ref[...]

    h2 = h2_ref[...]
    acc = x1_ref[...]
    for c in range(dff // fc):
        halves = []
        for off in (c * fc, dff + c * fc):
            u = _dot(h2, wup_ref[:, off:off + fc])
            ext[0:H, :] = tail[:, off:off + fc]
            ext[H:H + tm, :] = u
            cw = cw_ref[:, off:off + fc]
            cv = (cb_ref[:, off:off + fc] + cw[0:1, :] * ext[H - 2 * R:H - 2 * R + tm, :]
                  + cw[1:2, :] * ext[H - R:H - R + tm, :] + cw[2:3, :] * u)
            tail[:, off:off + fc] = ext[tm:tm + H, :]
            halves.append(cv)
        act = (_gelu(halves[0]) * halves[1]).astype(BF16)
        acc = acc + _dot(act, wdn_ref[c * fc:(c + 1) * fc, :])
    if final_norm:
        acc = _rms(acc, gfin_ref[...])
    x2_ref[...] = acc

    @pl.when(i == nt - 1)
    def _():
        tail_ref[...] = tail[...]


def _ffn(h2, x1, halo, wup, cw, cb, wdn, gfin, *, R, tm, final_norm):
    G, L, D = x1.shape
    F2 = wup.shape[1]
    H = halo.shape[1]
    fc = 256
    consts = (wup, cw, cb, wdn, gfin)
    return pl.pallas_call(
        functools.partial(_ffn_kernel, R=R, tm=tm, H=H, fc=fc, final_norm=final_norm),
        grid=(G, L // tm),
        in_specs=[pl.BlockSpec((None, tm, D), lambda g, i: (g, i, 0)),
                  pl.BlockSpec((None, tm, D), lambda g, i: (g, i, 0)),
                  pl.BlockSpec((None, H, F2), lambda g, i: (g, 0, 0))] + [_resident(a.shape) for a in consts],
        out_specs=(pl.BlockSpec((None, tm, D), lambda g, i: (g, i, 0)),
                   pl.BlockSpec((None, H, F2), lambda g, i: (g, 0, 0))),
        out_shape=(jax.ShapeDtypeStruct((G, L, D), F32), jax.ShapeDtypeStruct((G, H, F2), F32)),
        scratch_shapes=[pltpu.VMEM((H + tm, fc), F32), pltpu.VMEM((H, F2), F32)],
        compiler_params=_params("arbitrary", "arbitrary"),
        name="ffn",
    )(h2, x1, halo, *consts)


def _blockdiag(w):
    G, a, b = w.shape
    eye = jnp.eye(G, dtype=w.dtype)
    return (w[:, :, None, :] * eye[:, None, :, None]).reshape(G * a, G * b)


def _layer_weights(l, w):
    d = w["w_in"].shape[1]
    o_f = 3 * ATT_WIDTH
    o_s = o_f + ATT_HEADS
    o_g = o_s + S5_WIDTH + POOL_WIDTH
    w_in = w["w_in"][l]
    row = lambda a: a.reshape(1, -1).astype(F32)
    bblk = jnp.concatenate([_blockdiag(jnp.swapaxes(w["s5_b_re"][l], 1, 2)),
                            _blockdiag(jnp.swapaxes(w["s5_b_im"][l], 1, 2))], axis=1)
    cblk = jnp.concatenate([_blockdiag(jnp.swapaxes(w["s5_c_re"][l], 1, 2)),
                            -_blockdiag(jnp.swapaxes(w["s5_c_im"][l], 1, 2))], axis=0)
    return dict(
        g1=row(w["norm_mix_g"][l]),
        wqkv=w_in[:, :o_f].astype(BF16),
        wf=jnp.pad(w_in[:, o_f:o_s], ((0, 0), (0, LANES - ATT_HEADS))).astype(BF16),
        wsp=w_in[:, o_s:o_g].astype(BF16),
        wg=w_in[:, o_g:].astype(BF16),
        bf=jnp.pad(row(w["b_f"][l]), ((0, 0), (0, LANES - ATT_HEADS))),
        lre=row(w["s5_lam_re"][l]), lim=row(w["s5_lam_im"][l]),
        ldt=row(jnp.broadcast_to(w["s5_log_dt"][l][:, None], (S5_GROUPS, S5_STATE))),
        bblk=bblk.astype(BF16), cblk=cblk.astype(BF16), d=row(w["s5_d"][l]),
        w1=w["s5_glu_w1"][l].astype(BF16), b1=row(w["s5_glu_b1"][l]),
        w2=w["s5_glu_w2"][l].astype(BF16), b2=row(w["s5_glu_b2"][l]),
        pw=_blockdiag(w["pool_w"][l]).astype(BF16), ps=row(w["pool_scale"][l]),
        wa=w["w_br_a"][l].astype(BF16), wb=w["w_br_b"][l].astype(BF16), wc=w["w_br_c"][l].astype(BF16),
        wo=w["w_out"][l].astype(BF16), g2=row(w["norm_ffn_g"][l]),
        wup=w["w_up"][l].astype(BF16), cw=w["conv_w"][l].astype(F32), cb=row(w["conv_b"][l]),
        wdn=w["w_down"][l].astype(BF16),
    )


def _tile(n, pref):
    return pref if n % pref == 0 else n


def kernel(x_prompt, x_sample, cache_k, cache_v, cache_logf, state_ssm_re, state_ssm_im, state_pool,
           state_ffn_conv, page_table, norm_mix_g, w_in, b_f, s5_lam_re, s5_lam_im, s5_log_dt, s5_b_re, s5_b_im,
           s5_c_re, s5_c_im, s5_d, s5_glu_w1, s5_glu_b1, s5_glu_w2, s5_glu_b2, pool_w, pool_scale, w_br_a,
           w_br_b, w_br_c, w_out, norm_ffn_g, w_up, conv_w, conv_b, w_down, norm_final_g):
    w = dict(norm_mix_g=norm_mix_g, w_in=w_in, b_f=b_f, s5_lam_re=s5_lam_re, s5_lam_im=s5_lam_im,
             s5_log_dt=s5_log_dt, s5_b_re=s5_b_re, s5_b_im=s5_b_im, s5_c_re=s5_c_re, s5_c_im=s5_c_im, s5_d=s5_d,
             s5_glu_w1=s5_glu_w1, s5_glu_b1=s5_glu_b1, s5_glu_w2=s5_glu_w2, s5_glu_b2=s5_glu_b2, pool_w=pool_w,
             pool_scale=pool_scale, w_br_a=w_br_a, w_br_b=w_br_b, w_br_c=w_br_c, w_out=w_out,
             norm_ffn_g=norm_ffn_g, w_up=w_up, conv_w=conv_w, conv_b=conv_b, w_down=w_down)
    depth = w_in.shape[0]
    B, L, D = x_prompt.shape
    Bd, T, _ = x_sample.shape
    F2 = w_up.shape[2]
    n_pool, page = cache_k.shape[1], cache_k.shape[2]
    n2 = 2 * S5_N
    gfin = norm_final_g.reshape(1, D).astype(F32)

    tm = _tile(L, 512)
    halves = 2
    Lh = L // halves
    Rp = halves * B
    assert Rp == SUBLANES, "prompt S5 scan keeps one (half, batch) group per vreg"
    Tt = _tile(Lh, 64)
    Ns = Bd * T

    ck = cache_k.reshape(depth, n_pool, page, ATT_WIDTH)
    cv = cache_v.reshape(depth, n_pool, page, ATT_WIDTH)
    clft = jnp.swapaxes(cache_logf, 2, 3)
    pool_halo_s = jnp.pad(jnp.swapaxes(state_pool, 1, 2), ((0, 0), (1, 0), (0, 0), (0, 0)))
    pool_halo_s = pool_halo_s.reshape(depth, 1, POOL_HALO * Bd, POOL_WIDTH)
    conv_halo_s = jnp.swapaxes(state_ffn_conv, 1, 2).reshape(depth, 1, (CONV_TAPS - 1) * Bd, F2)
    h0_s = jnp.concatenate([state_ssm_re.reshape(depth, Bd, S5_N), state_ssm_im.reshape(depth, Bd, S5_N)], axis=-1)
    h0_p = jnp.zeros((Rp, n2), F32)
    pool_halo_p = jnp.zeros((B, POOL_HALO, POOL_WIDTH), F32)
    conv_halo_p = jnp.zeros((B, SUBLANES, F2), F32)

    xp = x_prompt
    xs = jnp.swapaxes(x_sample, 0, 1).reshape(1, Ns, D)
    st_p, st_s = [], []
    for l in range(depth):
        lw = _layer_weights(l, w)
        last = l == depth - 1

        q, k, v, kb, vb, lf, c, us, up = _proj(xp, lw["g1"], lw["wqkv"], lw["wf"], lw["wsp"], lw["bf"], tm)
        ct = jnp.swapaxes(c, 1, 2).reshape(B, ATT_HEADS // 2, 2, L)
        ya = _attn(q, kb, vb, ct, tm)
        u_perm = us.reshape(B, halves, Lh, S5_WIDTH).transpose(2, 1, 0, 3).reshape(Lh * Rp, S5_WIDTH)
        ys_perm, hfin = _s5(u_perm, h0_p, lw["lre"], lw["lim"], lw["ldt"], lw["bblk"], lw["cblk"], lw["d"],
                            lw["w1"], lw["b1"], lw["w2"], lw["b2"], R=Rp, Tt=Tt, two_pass=True)
        ys = ys_perm.reshape(Lh, halves, B, S5_WIDTH).transpose(2, 1, 0, 3).reshape(B * L, S5_WIDTH)
        yp = _pool(up, pool_halo_p, lw["pw"], lw["ps"], R=1, tm=tm, start_pos=0)
        x1, h2 = _merge(xp.reshape(B * L, D), ya.reshape(B * L, ATT_WIDTH), ys, yp.reshape(B * L, POOL_WIDTH),
                        lw["g1"], lw["wg"], lw["wa"], lw["wb"], lw["wc"], lw["wo"], lw["g2"], tm)
        xp, tail_p = _ffn(h2.reshape(B, L, D), x1.reshape(B, L, D), conv_halo_p, lw["wup"], lw["cw"], lw["cb"],
                          lw["wdn"], gfin, R=1, tm=tm, final_norm=last)
        st_p.append((k.reshape(B, L, ATT_HEADS, HEAD_DIM), v.reshape(B, L, ATT_HEADS, HEAD_DIM), lf,
                     hfin[B:, :S5_N].reshape(B, S5_GROUPS, S5_STATE), hfin[B:, S5_N:].reshape(B, S5_GROUPS, S5_STATE),
                     up[:, L - POOL_BUF:, :], tail_p[:, SUBLANES - (CONV_TAPS - 1):, :]))

        q, k, v, _, _, lf, _, us, up = _proj(xs, lw["g1"], lw["wqkv"], lw["wf"], lw["wsp"], lw["bf"], Ns)
        bt = lambda a: jnp.swapaxes(a.reshape(T, Bd, -1), 0, 1)
        k_b, v_b, lf_b = bt(k), bt(v), bt(lf)
        lfn = jnp.pad(jnp.swapaxes(lf_b, 1, 2), ((0, 0), (0, 0), (0, LANES - T)))
        ya = _attn_sample(page_table, bt(q), k_b, v_b, lfn, ck, cv, clft, l)
        ya = jnp.swapaxes(ya, 0, 1).reshape(Ns, ATT_WIDTH)
        ys, hfin = _s5(us.reshape(Ns, S5_WIDTH), h0_s[l], lw["lre"], lw["lim"], lw["ldt"], lw["bblk"], lw["cblk"],
                       lw["d"], lw["w1"], lw["b1"], lw["w2"], lw["b2"], R=Bd, Tt=T, two_pass=False)
        yp = _pool(up, pool_halo_s[l], lw["pw"], lw["ps"], R=Bd, tm=Ns, start_pos=page_table.shape[1] * page)
        x1, h2 = _merge(xs.reshape(Ns, D), ya, ys, yp.reshape(Ns, POOL_WIDTH),
                        lw["g1"], lw["wg"], lw["wa"], lw["wb"], lw["wc"], lw["wo"], lw["g2"], Ns)
        xs, tail_s = _ffn(h2.reshape(1, Ns, D), x1.reshape(1, Ns, D), conv_halo_s[l], lw["wup"], lw["cw"], lw["cb"],
                          lw["wdn"], gfin, R=Bd, tm=Ns, final_norm=last)
        new_pool = jnp.concatenate([state_pool[l][:, T:, :], bt(up)], axis=1)
        new_conv = jnp.swapaxes(tail_s.reshape(CONV_TAPS - 1, Bd, F2), 0, 1)
        st_s.append((k_b.reshape(Bd, T, ATT_HEADS, HEAD_DIM), v_b.reshape(Bd, T, ATT_HEADS, HEAD_DIM), lf_b,
                     hfin[:, :S5_N].reshape(Bd, S5_GROUPS, S5_STATE), hfin[:, S5_N:].reshape(Bd, S5_GROUPS, S5_STATE),
                     new_pool, new_conv))

    y_prompt = xp
    y_sample = jnp.swapaxes(xs.reshape(T, Bd, D), 0, 1)
    outs_p = [jnp.stack(z) for z in zip(*st_p)]
    outs_s = [jnp.stack(z) for z in zip(*st_s)]
    return (y_prompt, y_sample, *outs_p, *outs_s)
```

```python
import functools

import jax
import jax.numpy as jnp
from jax import lax
from jax.experimental import pallas as pl
from jax.experimental.pallas import tpu as pltpu

F32 = jnp.float32
BF16 = jnp.bfloat16

ATT_HEADS = 8
HEAD_DIM = 64
ATT_WIDTH = ATT_HEADS * HEAD_DIM
S5_WIDTH = 256
S5_GROUPS = 16
S5_STATE = 64
S5_N = S5_GROUPS * S5_STATE
POOL_WIDTH = 256
POOL_WINDOWS = (2, 4, 8, 16)
POOL_GW = 64
POOL_BUF = 15
POOL_HALO = 16
CONV_TAPS = 3
RMS_EPS = 1e-6
NEG_INF = -1e30
LANES = 128
SUBLANES = 8
VMEM_LIMIT = 56 * 1024 * 1024
HIGHEST = lax.Precision.HIGHEST
LOG2E = 1.4426950408889634


def _params(*sem):
    return pltpu.CompilerParams(dimension_semantics=sem, vmem_limit_bytes=VMEM_LIMIT)


def _resident(shape):
    nd = len(shape)
    return pl.BlockSpec(shape, lambda *_: (0,) * nd, pipeline_mode=pl.Buffered(1))


def _rms(x, g):
    return x * lax.rsqrt(jnp.mean(x * x, axis=-1, keepdims=True) + RMS_EPS) * g


def _sigmoid(x):
    return 1.0 / (1.0 + jnp.exp(-x))


def _gelu(x):
    return 0.5 * x * (1.0 + jnp.tanh(0.7978845608028654 * (x + 0.044715 * (x * x * x))))


def _log2(n):
    assert n & (n - 1) == 0, n
    return n.bit_length() - 1


def _dot(a, b):
    return jnp.dot(a, b, preferred_element_type=F32)


def _dot_nt(a, b):
    return lax.dot_general(a, b, (((1,), (1,)), ((), ())), preferred_element_type=F32)


def _proj_kernel(x_ref, g_ref, wqkv_ref, wf_ref, wsp_ref, bf_ref,
                 q_ref, k_ref, v_ref, kb_ref, vb_ref, lf_ref, c_ref, us_ref, up_ref,
                 carry_ref, *, tm):
    i = pl.program_id(1)
    h = _rms(x_ref[...], g_ref[...]).astype(BF16)
    qkv = _dot(h, wqkv_ref[...])
    q_ref[...] = (qkv[:, :ATT_WIDTH] * (LOG2E * HEAD_DIM ** -0.5)).astype(BF16)
    k = qkv[:, ATT_WIDTH:2 * ATT_WIDTH]
    v = qkv[:, 2 * ATT_WIDTH:]
    k_ref[...] = k
    v_ref[...] = v
    kb_ref[...] = k.astype(BF16)
    vb_ref[...] = v.astype(BF16)
    f = _dot(h, wf_ref[...]) + bf_ref[...]
    lf = -(jnp.maximum(-f, 0.0) + jnp.log1p(jnp.exp(-jnp.abs(f))))
    lf_ref[...] = lf[:, :ATT_HEADS]

    @pl.when(i == 0)
    def _():
        carry_ref[...] = jnp.zeros_like(carry_ref)

    row = lax.broadcasted_iota(jnp.int32, (tm, tm), 0)
    col = lax.broadcasted_iota(jnp.int32, (tm, tm), 1)
    tri = (row >= col).astype(F32)
    c = carry_ref[...] + jnp.dot(tri, lf, precision=HIGHEST, preferred_element_type=F32)
    c_ref[...] = c[:, :ATT_HEADS] * LOG2E
    carry_ref[...] = c[tm - 1:tm, :]
    sp = _dot(h, wsp_ref[...])
    us_ref[...] = sp[:, :S5_WIDTH]
    up_ref[...] = sp[:, S5_WIDTH:]


def _proj(x, g, wqkv, wf, wsp, bf, tm):
    B, L, D = x.shape
    nt = L // tm
    row = lambda w: pl.BlockSpec((None, tm, w), lambda b, i: (b, i, 0))
    out_shape = (
        jax.ShapeDtypeStruct((B, L, ATT_WIDTH), BF16),
        jax.ShapeDtypeStruct((B, L, ATT_WIDTH), F32),
        jax.ShapeDtypeStruct((B, L, ATT_WIDTH), F32),
        jax.ShapeDtypeStruct((B, L, ATT_WIDTH), BF16),
        jax.ShapeDtypeStruct((B, L, ATT_WIDTH), BF16),
        jax.ShapeDtypeStruct((B, L, ATT_HEADS), F32),
        jax.ShapeDtypeStruct((B, L, ATT_HEADS), F32),
        jax.ShapeDtypeStruct((B, L, S5_WIDTH), F32),
        jax.ShapeDtypeStruct((B, L, POOL_WIDTH), F32),
    )
    return pl.pallas_call(
        functools.partial(_proj_kernel, tm=tm),
        grid=(B, nt),
        in_specs=[row(D), _resident(g.shape), _resident(wqkv.shape), _resident(wf.shape),
                  _resident(wsp.shape), _resident(bf.shape)],
        out_specs=(row(ATT_WIDTH),) * 5 + (row(ATT_HEADS),) * 2 + (row(S5_WIDTH), row(POOL_WIDTH)),
        out_shape=out_shape,
        scratch_shapes=[pltpu.VMEM((1, LANES), F32)],
        compiler_params=_params("arbitrary", "arbitrary"),
        name="proj",
    )(x, g, wqkv, wf, wsp, bf)


def _attn_kernel(q_ref, k_ref, v_ref, c_ref, o_ref, qm_ref, s_ref, m_ref, acc_ref, *, tq):
    qi = pl.program_id(2)
    q2 = q_ref[...]
    lo = lax.broadcasted_iota(jnp.int32, (tq, LANES), 1) < HEAD_DIM
    zero = jnp.zeros_like(q2)
    qm_ref[0] = jnp.where(lo, q2, zero)
    qm_ref[1] = jnp.where(lo, zero, q2)
    m_ref[...] = jnp.full_like(m_ref, NEG_INF)
    acc_ref[...] = jnp.zeros_like(acc_ref)

    def scores(e, j, slot):
        start = pl.multiple_of(j * tq, tq)
        s_ref[slot, e] = _dot_nt(qm_ref[e], k_ref[pl.ds(start, tq), :]) - c_ref[e:e + 1, pl.ds(start, tq)]

    def update(e, j, slot, masked):
        s = s_ref[slot, e]
        if masked:
            row = lax.broadcasted_iota(jnp.int32, (tq, tq), 0)
            col = lax.broadcasted_iota(jnp.int32, (tq, tq), 1)
            s = jnp.where(col <= row, s, NEG_INF)
        m_old = m_ref[e]
        m_new = jnp.maximum(m_old, jnp.max(s, axis=1, keepdims=True))
        alpha = jnp.exp2(m_old - m_new)
        p = jnp.exp2(s - jnp.concatenate([m_new] * (tq // LANES), axis=1))
        start = pl.multiple_of(j * tq, tq)
        vj = v_ref[pl.ds(start, tq), :]
        one = jnp.ones_like(vj)
        vje = jnp.where(lo, vj, one) if e == 0 else jnp.where(lo, one, vj)
        acc_ref[e] = alpha * acc_ref[e] + _dot(p.astype(BF16), vje)
        m_ref[e] = m_new

    scores(0, 0, 0)
    scores(1, 0, 0)

    def step(j, slot):
        for e in range(2):
            scores(e, j + 1, 1 - slot)
            update(e, j, slot, False)

    def body(jj, carry):
        step(2 * jj, 0)
        step(2 * jj + 1, 1)
        return carry

    npair = jnp.right_shift(qi, 1)
    lax.fori_loop(0, npair, body, 0)

    @pl.when(qi == 2 * npair)
    def _():
        update(0, qi, 0, True)
        update(1, qi, 0, True)

    @pl.when(qi != 2 * npair)
    def _():
        step(qi - 1, 0)
        update(0, qi, 1, True)
        update(1, qi, 1, True)

    a0 = acc_ref[0]
    a1 = acc_ref[1]
    l0 = pltpu.roll(a0, HEAD_DIM, axis=1)
    l1 = pltpu.roll(a1, HEAD_DIM, axis=1)
    o_ref[...] = jnp.where(lo, a0 / l0, a1 / l1).astype(BF16)


def _attn(q, kb, vb, ct, tq):
    B, L, _ = q.shape
    nq = L // tq
    npair = ATT_HEADS // 2
    return pl.pallas_call(
        functools.partial(_attn_kernel, tq=tq),
        grid=(B, npair, nq),
        in_specs=[
            pl.BlockSpec((None, tq, LANES), lambda b, hp, qi: (b, qi, hp)),
            pl.BlockSpec((None, L, LANES), lambda b, hp, qi: (b, 0, hp)),
            pl.BlockSpec((None, L, LANES), lambda b, hp, qi: (b, 0, hp)),
            pl.BlockSpec((None, None, 2, L), lambda b, hp, qi: (b, hp, 0, 0)),
        ],
        out_specs=pl.BlockSpec((None, tq, LANES), lambda b, hp, qi: (b, qi, hp)),
        out_shape=jax.ShapeDtypeStruct((B, L, ATT_WIDTH), BF16),
        scratch_shapes=[pltpu.VMEM((2, tq, LANES), BF16), pltpu.VMEM((2, 2, tq, tq), F32),
                        pltpu.VMEM((2, tq, LANES), F32), pltpu.VMEM((2, tq, LANES), F32)],
        compiler_params=_params("arbitrary", "arbitrary", "arbitrary"),
        name="attn_prompt",
    )(q, kb, vb, ct)


def _split3_dot(a, b):
    a1 = a.astype(BF16)
    r1 = a - a1.astype(F32)
    a2 = r1.astype(BF16)
    a3 = (r1 - a2.astype(F32)).astype(BF16)
    return _dot(a1, b) + _dot(a2, b) + _dot(a3, b)


def _attn_sample_kernel(pt_ref, q_ref, kn_ref, vn_ref, lfn_ref, u_ref, e_ref, *rest, n_pages, page, t_new):
    del pt_ref
    k_refs = rest[:n_pages]
    v_refs = rest[n_pages:2 * n_pages]
    lf_refs = rest[2 * n_pages:3 * n_pages]
    o_ref = rest[3 * n_pages]
    nh = ATT_HEADS
    nrow = t_new * nh
    nkey = page * nh

    lf = jnp.concatenate([r[...] for r in lf_refs], axis=0)
    local = _split3_dot(lf, u_ref[...])
    carry = jnp.zeros((nh, 1), F32)
    cum = []
    for j in range(n_pages):
        cj = local[j * nh:(j + 1) * nh, :] + carry
        carry = cj[:, page - 1:page]
        cum.append(cj)
    cum = jnp.concatenate(cum, axis=0) * LOG2E
    bias = _split3_dot(cum, e_ref[...])

    q = q_ref[...]
    row_h = jnp.bitwise_and(lax.broadcasted_iota(jnp.int32, (nrow, nkey), 0), nh - 1)
    lane_h = jnp.bitwise_and(lax.broadcasted_iota(jnp.int32, (nrow, nkey), 1), nh - 1)
    same_head = row_h == lane_h
    s_pages = []
    for j in range(n_pages):
        s = _dot_nt(q, k_refs[j][...].astype(BF16))
        bj = jnp.concatenate([bias[j * nh:(j + 1) * nh, :]] * t_new, axis=0)
        s_pages.append(jnp.where(same_head, s - bj, NEG_INF))

    s_n = _dot_nt(q, kn_ref[...].astype(BF16))
    lfn = lfn_ref[...]
    lane_t = jnp.right_shift(lax.broadcasted_iota(jnp.int32, (nh, nrow), 1), _log2(nh))
    bias_n = jnp.zeros((nh, nrow), F32)
    for t in range(t_new):
        carry = carry + lfn[:, t:t + 1]
        bias_n = jnp.where(lane_t == t, carry * LOG2E, bias_n)
    bias_n = jnp.concatenate([bias_n] * t_new, axis=0)
    r_i = lax.broadcasted_iota(jnp.int32, (nrow, nrow), 0)
    c_i = lax.broadcasted_iota(jnp.int32, (nrow, nrow), 1)
    visible = ((jnp.bitwise_and(r_i, nh - 1) == jnp.bitwise_and(c_i, nh - 1))
               & (jnp.right_shift(c_i, _log2(nh)) <= jnp.right_shift(r_i, _log2(nh))))
    s_n = jnp.where(visible, s_n - bias_n, NEG_INF)

    m = jnp.max(s_n, axis=1, keepdims=True)
    for s in s_pages:
        m = jnp.maximum(m, jnp.max(s, axis=1, keepdims=True))
    p_n = jnp.exp2(s_n - m)
    l = jnp.sum(p_n, axis=1, keepdims=True)
    acc = _dot(p_n.astype(BF16), vn_ref[...].astype(BF16))
    for j in range(n_pages):
        p = jnp.exp2(s_pages[j] - m)
        l = l + jnp.sum(p, axis=1, keepdims=True)
        acc = acc + _dot(p.astype(BF16), v_refs[j][...].astype(BF16))
    o_ref[...] = (acc / l).astype(BF16)


def _attn_sample(page_table, q, kn, vn, lfn, upper, stretch, cache_k, cache_v, cache_lft, layer):
    Bd, nrow, _ = q.shape
    n_pages = page_table.shape[1]
    page = cache_lft.shape[3]

    def page_spec(shape, j):
        return pl.BlockSpec((None, None) + shape, lambda b, pt: (layer, pt[b, j], 0, 0))

    seq = lambda shape: pl.BlockSpec((None,) + shape, lambda b, pt: (b, 0, 0))
    const = lambda a: pl.BlockSpec(a.shape, lambda b, pt: (0, 0), pipeline_mode=pl.Buffered(1))
    in_specs = [seq((nrow, HEAD_DIM))] * 3 + [seq((ATT_HEADS, LANES)), const(upper), const(stretch)]
    in_specs += [page_spec((page * ATT_HEADS, HEAD_DIM), j) for j in range(n_pages)]
    in_specs += [page_spec((page * ATT_HEADS, HEAD_DIM), j) for j in range(n_pages)]
    in_specs += [page_spec((ATT_HEADS, page), j) for j in range(n_pages)]
    grid_spec = pltpu.PrefetchScalarGridSpec(
        num_scalar_prefetch=1, grid=(Bd,), in_specs=in_specs,
        out_specs=pl.BlockSpec((None, nrow, HEAD_DIM), lambda b, pt: (b, 0, 0)))
    return pl.pallas_call(
        functools.partial(_attn_sample_kernel, n_pages=n_pages, page=page, t_new=nrow // ATT_HEADS),
        grid_spec=grid_spec,
        out_shape=jax.ShapeDtypeStruct((Bd, nrow, HEAD_DIM), BF16),
        compiler_params=_params("arbitrary"),
        name="attn_sample",
    )(page_table, q, kn, vn, lfn, upper, stretch,
      *([cache_k] * n_pages), *([cache_v] * n_pages), *([cache_lft] * n_pages))


def _s5_kernel(u_ref, h0_ref, lre_ref, lim_ref, ldt_ref, bblk_ref, cblk_ref, d_ref,
               w1_ref, b1_ref, w2_ref, b2_ref, y_ref, hfin_ref,
               x_scr, h_scr, st_scr, end_scr, *, R, Tt, two_pass):
    p = pl.program_id(0)
    i = pl.program_id(1)
    nt = pl.num_programs(1)
    n = S5_N
    last_pass = 1 if two_pass else 0

    dt = jnp.exp(ldt_ref[...])
    lr = lre_ref[...]
    li = lim_ref[...]
    mag = jnp.exp(lr * dt)
    a_re = mag * jnp.cos(li * dt)
    a_im = mag * jnp.sin(li * dt)
    den = lr * lr + li * li
    z_re = ((a_re - 1.0) * lr + a_im * li) / den
    z_im = (a_im * lr - (a_re - 1.0) * li) / den

    u = u_ref[...]
    bu = _dot(u.astype(BF16), bblk_ref[...])
    bur = bu[:, :n]
    bui = bu[:, n:]
    x_scr[:, :n] = z_re * bur - z_im * bui
    x_scr[:, n:] = z_re * bui + z_im * bur

    @pl.when((i == 0) & (p == 0))
    def _():
        st_scr[...] = h0_ref[...]

    if two_pass:
        @pl.when((i == 0) & (p == 1))
        def _():
            rolled = pltpu.roll(end_scr[...], R // 2, axis=0)
            first = lax.broadcasted_iota(jnp.int32, (R, 2 * n), 0) < R // 2
            st_scr[...] = jnp.where(first, h0_ref[...], rolled)

    ar = jnp.broadcast_to(a_re, (R, n))
    ai = jnp.broadcast_to(a_im, (R, n))

    def scan(store):
        def step(t, carry):
            hr, hi = carry
            r0 = t * R if isinstance(t, int) else pl.multiple_of(t * R, R)
            xr = x_scr[pl.ds(r0, R), :n]
            xi = x_scr[pl.ds(r0, R), n:]
            nr = ar * hr - ai * hi + xr
            ni = ar * hi + ai * hr + xi
            if store:
                h_scr[pl.ds(r0, R), :n] = nr
                h_scr[pl.ds(r0, R), n:] = ni
            return nr, ni

        if R == SUBLANES:
            hr, hi = lax.fori_loop(0, Tt, step, (st_scr[:, :n], st_scr[:, n:]), unroll=2)
        else:
            hr, hi = st_scr[:, :n], st_scr[:, n:]
            for t in range(Tt):
                hr, hi = step(t, (hr, hi))
        st_scr[:, :n] = hr
        st_scr[:, n:] = hi

    if two_pass:
        @pl.when(p == 0)
        def _():
            scan(False)

            @pl.when(i == nt - 1)
            def _():
                end_scr[...] = st_scr[...]

    @pl.when(p == last_pass)
    def _():
        scan(True)
        y = _dot(h_scr[...].astype(BF16), cblk_ref[...]) + d_ref[...] * u
        z = _gelu(y).astype(BF16)
        out = (_dot(z, w1_ref[...]) + b1_ref[...]) * _sigmoid(_dot(z, w2_ref[...]) + b2_ref[...])
        y_ref[...] = out.astype(BF16)

        @pl.when(i == nt - 1)
        def _():
            hfin_ref[...] = st_scr[...]


def _s5(u, h0, lre, lim, ldt, bblk, cblk, d, w1, b1, w2, b2, *, R, Tt, two_pass):
    rows = u.shape[0]
    tile = R * Tt
    nt = rows // tile
    npass = 2 if two_pass else 1
    n2 = 2 * S5_N
    consts = (h0, lre, lim, ldt, bblk, cblk, d, w1, b1, w2, b2)
    return pl.pallas_call(
        functools.partial(_s5_kernel, R=R, Tt=Tt, two_pass=two_pass),
        grid=(npass, nt),
        in_specs=[pl.BlockSpec((tile, S5_WIDTH), lambda p, i: (i, 0))] + [_resident(a.shape) for a in consts],
        out_specs=(pl.BlockSpec((tile, S5_WIDTH), lambda p, i: (i * p if two_pass else i, 0)),
                   pl.BlockSpec((R, n2), lambda p, i: (0, 0))),
        out_shape=(jax.ShapeDtypeStruct((rows, S5_WIDTH), BF16), jax.ShapeDtypeStruct((R, n2), F32)),
        scratch_shapes=[pltpu.VMEM((tile, n2), F32), pltpu.VMEM((tile, n2), F32),
                        pltpu.VMEM((R, n2), F32), pltpu.VMEM((R, n2), F32)],
        compiler_params=_params("arbitrary", "arbitrary"),
        name="s5",
    )(u, *consts)


def _pool_kernel(p_ref, halo_ref, w_ref, sc_ref, y_ref, ext, a2, a4, a8, a16, *, R, tm, start_pos, nt):
    i = pl.program_id(1)
    H = POOL_HALO * R
    n = H + tm

    @pl.when(i == 0)
    def _():
        ext[0:H, :] = halo_ref[...]

    x = p_ref[...]
    ext[H:n, :] = x
    n2, n4, n8, n16 = n - R, n - 3 * R, n - 7 * R, n - 15 * R
    a2[0:n2, :] = ext[R:R + n2, :] + ext[0:n2, :]
    a4[0:n4, :] = a2[2 * R:2 * R + n4, :] + a2[0:n4, :]
    a8[0:n8, :] = a4[4 * R:4 * R + n8, :] + a4[0:n8, :]
    a16[0:n16, :] = a8[8 * R:8 * R + n16, :] + a8[0:n16, :]
    s2 = a2[15 * R:15 * R + tm, :]
    s4 = a4[13 * R:13 * R + tm, :]
    s8 = a8[9 * R:9 * R + tm, :]
    s16 = a16[R:R + tm, :]
    pos = start_pos + jnp.right_shift(i * tm + lax.broadcasted_iota(jnp.int32, (tm, POOL_WIDTH), 0), _log2(R))
    lane = lax.broadcasted_iota(jnp.int32, (tm, POOL_WIDTH), 1)
    cnt = lambda w: jnp.minimum(w, pos + 1).astype(F32)
    mean = jnp.where(lane < POOL_GW, s2 / cnt(2),
                     jnp.where(lane < 2 * POOL_GW, s4 / cnt(4),
                               jnp.where(lane < 3 * POOL_GW, s8 / cnt(8), s16 / cnt(16))))
    m = (mean - x).astype(BF16)
    y_ref[...] = (_dot(m, w_ref[...]) * sc_ref[...]).astype(BF16)
    if nt > 1:
        ext[0:H, :] = ext[tm:tm + H, :]


def _pool(p, halo, wblk, scale, *, R, tm, start_pos):
    G, L, W = p.shape
    nt = L // tm
    H = POOL_HALO * R
    buf = pltpu.VMEM((H + tm, W), F32)
    return pl.pallas_call(
        functools.partial(_pool_kernel, R=R, tm=tm, start_pos=start_pos, nt=nt),
        grid=(G, nt),
        in_specs=[pl.BlockSpec((None, tm, W), lambda g, i: (g, i, 0)),
                  pl.BlockSpec((None, H, W), lambda g, i: (g, 0, 0)),
                  _resident(wblk.shape), _resident(scale.shape)],
        out_specs=pl.BlockSpec((None, tm, W), lambda g, i: (g, i, 0)),
        out_shape=jax.ShapeDtypeStruct((G, L, W), BF16),
        scratch_shapes=[buf] * 5,
        compiler_params=_params("arbitrary", "arbitrary"),
        name="pool",
    )(p, halo, wblk, scale)


def _merge_kernel(x_ref, ya_ref, ys_ref, yp_ref, g1_ref, wg_ref, wa_ref, wb_ref, wc_ref, wo_ref, g2_ref,
                  x1_ref, h2_ref):
    x = x_ref[...]
    d = x.shape[-1]
    h = _rms(x, g1_ref[...]).astype(BF16)
    gates = _dot(h, wg_ref[...])
    merged = (_sigmoid(gates[:, :d]) * _dot(ya_ref[...], wa_ref[...])
              + _sigmoid(gates[:, d:2 * d]) * _dot(ys_ref[...], wb_ref[...])
              + _sigmoid(gates[:, 2 * d:]) * _dot(yp_ref[...], wc_ref[...]))
    x1 = x + _dot(merged.astype(BF16), wo_ref[...])
    x1_ref[...] = x1
    h2_ref[...] = _rms(x1, g2_ref[...]).astype(BF16)


def _merge(x, ya, ys, yp, g1, wg, wa, wb, wc, wo, g2, tm):
    N, D = x.shape
    row = lambda w: pl.BlockSpec((tm, w), lambda i: (i, 0))
    consts = (g1, wg, wa, wb, wc, wo, g2)
    return pl.pallas_call(
        _merge_kernel,
        grid=(N // tm,),
        in_specs=[row(D), row(ATT_WIDTH), row(S5_WIDTH), row(POOL_WIDTH)] + [_resident(a.shape) for a in consts],
        out_specs=(row(D), row(D)),
        out_shape=(jax.ShapeDtypeStruct((N, D), F32), jax.ShapeDtypeStruct((N, D), BF16)),
        compiler_params=_params("arbitrary"),
        name="merge",
    )(x, ya, ys, yp, *consts)


def _ffn_kernel(h2_ref, x1_ref, halo_ref, wup_ref, cw_ref, cb_ref, wdn_ref, gfin_ref,
                x2_ref, tail_ref, ext, tail, *, R, tm, H, fc, final_norm):
    i = pl.program_id(1)
    nt = pl.num_programs(1)
    dff = wdn_ref.shape[0]

    @pl.when(i == 0)
    def _():
        tail[...] = halo_ref[...]

    h2 = h2_ref[...]
    acc = x1_ref[...]
    for c in range(dff // fc):
        halves = []
        for off in (c * fc, dff + c * fc):
            u = _dot(h2, wup_ref[:, off:off + fc])
            ext[0:H, :] = tail[:, off:off + fc]
            ext[H:H + tm, :] = u
            cw = cw_ref[:, off:off + fc]
            cv = (cb_ref[:, off:off + fc] + cw[0:1, :] * ext[H - 2 * R:H - 2 * R + tm, :]
                  + cw[1:2, :] * ext[H - R:H - R + tm, :] + cw[2:3, :] * u)
            tail[:, off:off + fc] = ext[tm:tm + H, :]
            halves.append(cv)
        act = (_gelu(halves[0]) * halves[1]).astype(BF16)
        acc = acc + _dot(act, wdn_ref[c * fc:(c + 1) * fc, :])
    if final_norm:
        acc = _rms(acc, gfin_ref[...])
    x2_ref[...] = acc

    @pl.when(i == nt - 1)
    def _():
        tail_ref[...] = tail[...]


def _ffn(h2, x1, halo, wup, cw, cb, wdn, gfin, *, R, tm, final_norm):
    G, L, D = x1.shape
    F2 = wup.shape[1]
    H = halo.shape[1]
    fc = 256
    consts = (wup, cw, cb, wdn, gfin)
    return pl.pallas_call(
        functools.partial(_ffn_kernel, R=R, tm=tm, H=H, fc=fc, final_norm=final_norm),
        grid=(G, L // tm),
        in_specs=[pl.BlockSpec((None, tm, D), lambda g, i: (g, i, 0)),
                  pl.BlockSpec((None, tm, D), lambda g, i: (g, i, 0)),
                  pl.BlockSpec((None, H, F2), lambda g, i: (g, 0, 0))] + [_resident(a.shape) for a in consts],
        out_specs=(pl.BlockSpec((None, tm, D), lambda g, i: (g, i, 0)),
                   pl.BlockSpec((None, H, F2), lambda g, i: (g, 0, 0))),
        out_shape=(jax.ShapeDtypeStruct((G, L, D), F32), jax.ShapeDtypeStruct((G, H, F2), F32)),
        scratch_shapes=[pltpu.VMEM((H + tm, fc), F32), pltpu.VMEM((H, F2), F32)],
        compiler_params=_params("arbitrary", "arbitrary"),
        name="ffn",
    )(h2, x1, halo, *consts)


def _blockdiag(w):
    G, a, b = w.shape
    eye = jnp.eye(G, dtype=w.dtype)
    return (w[:, :, None, :] * eye[:, None, :, None]).reshape(G * a, G * b)


def _layer_weights(l, w):
    d = w["w_in"].shape[1]
    o_f = 3 * ATT_WIDTH
    o_s = o_f + ATT_HEADS
    o_g = o_s + S5_WIDTH + POOL_WIDTH
    w_in = w["w_in"][l]
    row = lambda a: a.reshape(1, -1).astype(F32)
    bblk = jnp.concatenate([_blockdiag(jnp.swapaxes(w["s5_b_re"][l], 1, 2)),
                            _blockdiag(jnp.swapaxes(w["s5_b_im"][l], 1, 2))], axis=1)
    cblk = jnp.concatenate([_blockdiag(jnp.swapaxes(w["s5_c_re"][l], 1, 2)),
                            -_blockdiag(jnp.swapaxes(w["s5_c_im"][l], 1, 2))], axis=0)
    return dict(
        g1=row(w["norm_mix_g"][l]),
        wqkv=w_in[:, :o_f].astype(BF16),
        wf=jnp.pad(w_in[:, o_f:o_s], ((0, 0), (0, LANES - ATT_HEADS))).astype(BF16),
        wsp=w_in[:, o_s:o_g].astype(BF16),
        wg=w_in[:, o_g:].astype(BF16),
        bf=jnp.pad(row(w["b_f"][l]), ((0, 0), (0, LANES - ATT_HEADS))),
        lre=row(w["s5_lam_re"][l]), lim=row(w["s5_lam_im"][l]),
        ldt=row(jnp.broadcast_to(w["s5_log_dt"][l][:, None], (S5_GROUPS, S5_STATE))),
        bblk=bblk.astype(BF16), cblk=cblk.astype(BF16), d=row(w["s5_d"][l]),
        w1=w["s5_glu_w1"][l].astype(BF16), b1=row(w["s5_glu_b1"][l]),
        w2=w["s5_glu_w2"][l].astype(BF16), b2=row(w["s5_glu_b2"][l]),
        pw=_blockdiag(w["pool_w"][l]).astype(BF16), ps=row(w["pool_scale"][l]),
        wa=w["w_br_a"][l].astype(BF16), wb=w["w_br_b"][l].astype(BF16), wc=w["w_br_c"][l].astype(BF16),
        wo=w["w_out"][l].astype(BF16), g2=row(w["norm_ffn_g"][l]),
        wup=w["w_up"][l].astype(BF16), cw=w["conv_w"][l].astype(F32), cb=row(w["conv_b"][l]),
        wdn=w["w_down"][l].astype(BF16),
    )


def _tile(n, pref):
    return pref if n % pref == 0 else n


def kernel(x_prompt, x_sample, cache_k, cache_v, cache_logf, state_ssm_re, state_ssm_im, state_pool,
           state_ffn_conv, page_table, norm_mix_g, w_in, b_f, s5_lam_re, s5_lam_im, s5_log_dt, s5_b_re, s5_b_im,
           s5_c_re, s5_c_im, s5_d, s5_glu_w1, s5_glu_b1, s5_glu_w2, s5_glu_b2, pool_w, pool_scale, w_br_a,
           w_br_b, w_br_c, w_out, norm_ffn_g, w_up, conv_w, conv_b, w_down, norm_final_g):
    w = dict(norm_mix_g=norm_mix_g, w_in=w_in, b_f=b_f, s5_lam_re=s5_lam_re, s5_lam_im=s5_lam_im,
             s5_log_dt=s5_log_dt, s5_b_re=s5_b_re, s5_b_im=s5_b_im, s5_c_re=s5_c_re, s5_c_im=s5_c_im, s5_d=s5_d,
             s5_glu_w1=s5_glu_w1, s5_glu_b1=s5_glu_b1, s5_glu_w2=s5_glu_w2, s5_glu_b2=s5_glu_b2, pool_w=pool_w,
             pool_scale=pool_scale, w_br_a=w_br_a, w_br_b=w_br_b, w_br_c=w_br_c, w_out=w_out,
             norm_ffn_g=norm_ffn_g, w_up=w_up, conv_w=conv_w, conv_b=conv_b, w_down=w_down)
    depth = w_in.shape[0]
    B, L, D = x_prompt.shape
    Bd, T, _ = x_sample.shape
    F2 = w_up.shape[2]
    n_pool, page = cache_k.shape[1], cache_k.shape[2]
    n2 = 2 * S5_N
    gfin = norm_final_g.reshape(1, D).astype(F32)

    tm = _tile(L, 512)
    halves = 2
    Lh = L // halves
    Rp = halves * B
    assert Rp == SUBLANES, "prompt S5 scan keeps one (half, batch) group per vreg"
    Tt = _tile(Lh, 64)
    Ns = Bd * T

    ck = cache_k.reshape(depth, n_pool, page * ATT_HEADS, HEAD_DIM)
    cv = cache_v.reshape(depth, n_pool, page * ATT_HEADS, HEAD_DIM)
    clft = jnp.swapaxes(cache_logf, 2, 3)
    pos_r = lax.broadcasted_iota(jnp.int32, (page, page), 0)
    pos_c = lax.broadcasted_iota(jnp.int32, (page, page), 1)
    upper = (pos_r <= pos_c).astype(BF16)
    key_pos = lax.broadcasted_iota(jnp.int32, (page, page * ATT_HEADS), 1) // ATT_HEADS
    stretch = (lax.broadcasted_iota(jnp.int32, (page, page * ATT_HEADS), 0) == key_pos).astype(BF16)
    pool_halo_s = jnp.pad(jnp.swapaxes(state_pool, 1, 2), ((0, 0), (1, 0), (0, 0), (0, 0)))
    pool_halo_s = pool_halo_s.reshape(depth, 1, POOL_HALO * Bd, POOL_WIDTH)
    conv_halo_s = jnp.swapaxes(state_ffn_conv, 1, 2).reshape(depth, 1, (CONV_TAPS - 1) * Bd, F2)
    h0_s = jnp.concatenate([state_ssm_re.reshape(depth, Bd, S5_N), state_ssm_im.reshape(depth, Bd, S5_N)], axis=-1)
    h0_p = jnp.zeros((Rp, n2), F32)
    pool_halo_p = jnp.zeros((B, POOL_HALO, POOL_WIDTH), F32)
    conv_halo_p = jnp.zeros((B, SUBLANES, F2), F32)

    xp = x_prompt
    xs = jnp.swapaxes(x_sample, 0, 1).reshape(1, Ns, D)
    st_p, st_s = [], []
    for l in range(depth):
        lw = _layer_weights(l, w)
        last = l == depth - 1

        q, k, v, kb, vb, lf, c, us, up = _proj(xp, lw["g1"], lw["wqkv"], lw["wf"], lw["wsp"], lw["bf"], tm)
        ct = jnp.swapaxes(c, 1, 2).reshape(B, ATT_HEADS // 2, 2, L)
        ya = _attn(q, kb, vb, ct, tm)
        u_perm = us.reshape(B, halves, Lh, S5_WIDTH).transpose(2, 1, 0, 3).reshape(Lh * Rp, S5_WIDTH)
        ys_perm, hfin = _s5(u_perm, h0_p, lw["lre"], lw["lim"], lw["ldt"], lw["bblk"], lw["cblk"], lw["d"],
                            lw["w1"], lw["b1"], lw["w2"], lw["b2"], R=Rp, Tt=Tt, two_pass=True)
        ys = ys_perm.reshape(Lh, halves, B, S5_WIDTH).transpose(2, 1, 0, 3).reshape(B * L, S5_WIDTH)
        yp = _pool(up, pool_halo_p, lw["pw"], lw["ps"], R=1, tm=tm, start_pos=0)
        x1, h2 = _merge(xp.reshape(B * L, D), ya.reshape(B * L, ATT_WIDTH), ys, yp.reshape(B * L, POOL_WIDTH),
                        lw["g1"], lw["wg"], lw["wa"], lw["wb"], lw["wc"], lw["wo"], lw["g2"], tm)
        xp, tail_p = _ffn(h2.reshape(B, L, D), x1.reshape(B, L, D), conv_halo_p, lw["wup"], lw["cw"], lw["cb"],
                          lw["wdn"], gfin, R=1, tm=tm, final_norm=last)
        st_p.append((k.reshape(B, L, ATT_HEADS, HEAD_DIM), v.reshape(B, L, ATT_HEADS, HEAD_DIM), lf,
                     hfin[B:, :S5_N].reshape(B, S5_GROUPS, S5_STATE), hfin[B:, S5_N:].reshape(B, S5_GROUPS, S5_STATE),
                     up[:, L - POOL_BUF:, :], tail_p[:, SUBLANES - (CONV_TAPS - 1):, :]))

        q, k, v, _, _, lf, _, us, up = _proj(xs, lw["g1"], lw["wqkv"], lw["wf"], lw["wsp"], lw["bf"], Ns)
        bt = lambda a: jnp.swapaxes(a.reshape(T, Bd, -1), 0, 1)
        k_b, v_b, lf_b = bt(k), bt(v), bt(lf)
        lfn = jnp.pad(jnp.swapaxes(lf_b, 1, 2), ((0, 0), (0, 0), (0, LANES - T)))
        th = lambda a: a.reshape(Bd, T * ATT_HEADS, HEAD_DIM)
        ya = _attn_sample(page_table, th(bt(q)), th(k_b), th(v_b), lfn, upper, stretch, ck, cv, clft, l)
        ya = jnp.swapaxes(ya.reshape(Bd, T, ATT_WIDTH), 0, 1).reshape(Ns, ATT_WIDTH)
        ys, hfin = _s5(us.reshape(Ns, S5_WIDTH), h0_s[l], lw["lre"], lw["lim"], lw["ldt"], lw["bblk"], lw["cblk"],
                       lw["d"], lw["w1"], lw["b1"], lw["w2"], lw["b2"], R=Bd, Tt=T, two_pass=False)
        yp = _pool(up, pool_halo_s[l], lw["pw"], lw["ps"], R=Bd, tm=Ns, start_pos=page_table.shape[1] * page)
        x1, h2 = _merge(xs.reshape(Ns, D), ya, ys, yp.reshape(Ns, POOL_WIDTH),
                        lw["g1"], lw["wg"], lw["wa"], lw["wb"], lw["wc"], lw["wo"], lw["g2"], Ns)
        xs, tail_s = _ffn(h2.reshape(1, Ns, D), x1.reshape(1, Ns, D), conv_halo_s[l], lw["wup"], lw["cw"], lw["cb"],
                          lw["wdn"], gfin, R=Bd, tm=Ns, final_norm=last)
        new_pool = jnp.concatenate([state_pool[l][:, T:, :], bt(up)], axis=1)
        new_conv = jnp.swapaxes(tail_s.reshape(CONV_TAPS - 1, Bd, F2), 0, 1)
        st_s.append((k_b.reshape(Bd, T, ATT_HEADS, HEAD_DIM), v_b.reshape(Bd, T, ATT_HEADS, HEAD_DIM), lf_b,
                     hfin[:, :S5_N].reshape(Bd, S5_GROUPS, S5_STATE), hfin[:, S5_N:].reshape(Bd, S5_GROUPS, S5_STATE),
                     new_pool, new_conv))

    y_prompt = xp
    y_sample = jnp.swapaxes(xs.reshape(T, Bd, D), 0, 1)
    outs_p = [jnp.stack(z) for z in zip(*st_p)]
    outs_s = [jnp.stack(z) for z in zip(*st_s)]
    return (y_prompt, y_sample, *outs_p, *outs_s)
```

```python
import functools

import jax
import jax.numpy as jnp
from jax import lax
from jax.experimental import pallas as pl
from jax.experimental.pallas import tpu as pltpu

F32 = jnp.float32
BF16 = jnp.bfloat16

ATT_HEADS = 8
HEAD_DIM = 64
ATT_WIDTH = ATT_HEADS * HEAD_DIM
S5_WIDTH = 256
S5_GROUPS = 16
S5_STATE = 64
S5_N = S5_GROUPS * S5_STATE
POOL_WIDTH = 256
POOL_WINDOWS = (2, 4, 8, 16)
POOL_GW = 64
POOL_BUF = 15
POOL_HALO = 16
CONV_TAPS = 3
RMS_EPS = 1e-6
NEG_INF = -1e30
LANES = 128
SUBLANES = 8
VMEM_LIMIT = 56 * 1024 * 1024
HIGHEST = lax.Precision.HIGHEST
LOG2E = 1.4426950408889634


def _params(*sem):
    return pltpu.CompilerParams(dimension_semantics=sem, vmem_limit_bytes=VMEM_LIMIT)


def _resident(shape):
    nd = len(shape)
    return pl.BlockSpec(shape, lambda *_: (0,) * nd, pipeline_mode=pl.Buffered(1))


def _rms(x, g):
    return x * lax.rsqrt(jnp.mean(x * x, axis=-1, keepdims=True) + RMS_EPS) * g


def _sigmoid(x):
    return 1.0 / (1.0 + jnp.exp(-x))


def _gelu(x):
    return 0.5 * x * (1.0 + jnp.tanh(0.7978845608028654 * (x + 0.044715 * (x * x * x))))


def _log2(n):
    assert n & (n - 1) == 0, n
    return n.bit_length() - 1


def _dot(a, b):
    return jnp.dot(a, b, preferred_element_type=F32)


def _dot_nt(a, b):
    return lax.dot_general(a, b, (((1,), (1,)), ((), ())), preferred_element_type=F32)


def _proj_kernel(x_ref, g_ref, wqkv_ref, wf_ref, wsp_ref, bf_ref,
                 q_ref, k_ref, v_ref, kb_ref, vb_ref, lf_ref, c_ref, us_ref, up_ref,
                 carry_ref, *, tm, sub, with_cumsum):
    i = pl.program_id(1)

    @pl.when(i == 0)
    def _():
        carry_ref[...] = jnp.zeros_like(carry_ref)

    row = lax.broadcasted_iota(jnp.int32, (sub, sub), 0)
    col = lax.broadcasted_iota(jnp.int32, (sub, sub), 1)
    tri = (row >= col).astype(F32)
    carry = carry_ref[...]
    for r in range(tm // sub):
        rows = pl.ds(r * sub, sub)
        h = _rms(x_ref[rows, :], g_ref[...]).astype(BF16)
        qkv = _dot(h, wqkv_ref[...])
        q_ref[rows, :] = (qkv[:, :ATT_WIDTH] * (LOG2E * HEAD_DIM ** -0.5)).astype(BF16)
        k = qkv[:, ATT_WIDTH:2 * ATT_WIDTH]
        v = qkv[:, 2 * ATT_WIDTH:]
        k_ref[rows, :] = k
        v_ref[rows, :] = v
        kb_ref[rows, :] = k.astype(BF16)
        vb_ref[rows, :] = v.astype(BF16)
        f = _dot(h, wf_ref[...]) + bf_ref[...]
        lf = -(jnp.maximum(-f, 0.0) + jnp.log1p(jnp.exp(-jnp.abs(f))))
        lf_ref[rows, :] = lf[:, :ATT_HEADS]
        if with_cumsum:
            c = carry + jnp.dot(tri, lf, precision=HIGHEST, preferred_element_type=F32)
            carry = c[sub - 1:sub, :]
        else:
            c = lf
        c_ref[rows, :] = c[:, :ATT_HEADS] * LOG2E
        sp = _dot(h, wsp_ref[...])
        us_ref[rows, :] = sp[:, :S5_WIDTH]
        up_ref[rows, :] = sp[:, S5_WIDTH:]
    carry_ref[...] = carry


def _proj(x, g, wqkv, wf, wsp, bf, tm, with_cumsum):
    B, L, D = x.shape
    nt = L // tm
    row = lambda w: pl.BlockSpec((None, tm, w), lambda b, i: (b, i, 0))
    out_shape = (
        jax.ShapeDtypeStruct((B, L, ATT_WIDTH), BF16),
        jax.ShapeDtypeStruct((B, L, ATT_WIDTH), F32),
        jax.ShapeDtypeStruct((B, L, ATT_WIDTH), F32),
        jax.ShapeDtypeStruct((B, L, ATT_WIDTH), BF16),
        jax.ShapeDtypeStruct((B, L, ATT_WIDTH), BF16),
        jax.ShapeDtypeStruct((B, L, ATT_HEADS), F32),
        jax.ShapeDtypeStruct((B, L, ATT_HEADS), F32),
        jax.ShapeDtypeStruct((B, L, S5_WIDTH), F32),
        jax.ShapeDtypeStruct((B, L, POOL_WIDTH), F32),
    )
    return pl.pallas_call(
        functools.partial(_proj_kernel, tm=tm, sub=min(tm, 256), with_cumsum=with_cumsum),
        grid=(B, nt),
        in_specs=[row(D), _resident(g.shape), _resident(wqkv.shape), _resident(wf.shape),
                  _resident(wsp.shape), _resident(bf.shape)],
        out_specs=(row(ATT_WIDTH),) * 5 + (row(ATT_HEADS),) * 2 + (row(S5_WIDTH), row(POOL_WIDTH)),
        out_shape=out_shape,
        scratch_shapes=[pltpu.VMEM((1, LANES), F32)],
        compiler_params=_params("arbitrary", "arbitrary"),
        name="proj",
    )(x, g, wqkv, wf, wsp, bf)


def _attn_kernel(q_ref, k_ref, v_ref, c_ref, o_ref, qm_ref, s_ref, m_ref, acc_ref, *, tq):
    qi = pl.program_id(2)
    q2 = q_ref[...]
    lo = lax.broadcasted_iota(jnp.int32, (tq, LANES), 1) < HEAD_DIM
    zero = jnp.zeros_like(q2)
    qm_ref[0] = jnp.where(lo, q2, zero)
    qm_ref[1] = jnp.where(lo, zero, q2)
    m_ref[...] = jnp.full_like(m_ref, NEG_INF)
    acc_ref[...] = jnp.zeros_like(acc_ref)

    def scores(e, j, slot):
        start = pl.multiple_of(j * tq, tq)
        s_ref[slot, e] = _dot_nt(qm_ref[e], k_ref[pl.ds(start, tq), :]) - c_ref[e:e + 1, pl.ds(start, tq)]

    def update(e, j, slot, masked):
        s = s_ref[slot, e]
        if masked:
            row = lax.broadcasted_iota(jnp.int32, (tq, tq), 0)
            col = lax.broadcasted_iota(jnp.int32, (tq, tq), 1)
            s = jnp.where(col <= row, s, NEG_INF)
        m_old = m_ref[e]
        m_new = jnp.maximum(m_old, jnp.max(s, axis=1, keepdims=True))
        alpha = jnp.exp2(m_old - m_new)
        p = jnp.exp2(s - jnp.concatenate([m_new] * (tq // LANES), axis=1))
        start = pl.multiple_of(j * tq, tq)
        vj = v_ref[pl.ds(start, tq), :]
        one = jnp.ones_like(vj)
        vje = jnp.where(lo, vj, one) if e == 0 else jnp.where(lo, one, vj)
        acc_ref[e] = alpha * acc_ref[e] + _dot(p.astype(BF16), vje)
        m_ref[e] = m_new

    scores(0, 0, 0)
    scores(1, 0, 0)

    def step(j, slot):
        for e in range(2):
            scores(e, j + 1, 1 - slot)
            update(e, j, slot, False)

    def body(jj, carry):
        step(2 * jj, 0)
        step(2 * jj + 1, 1)
        return carry

    npair = jnp.right_shift(qi, 1)
    lax.fori_loop(0, npair, body, 0)

    @pl.when(qi == 2 * npair)
    def _():
        update(0, qi, 0, True)
        update(1, qi, 0, True)

    @pl.when(qi != 2 * npair)
    def _():
        step(qi - 1, 0)
        update(0, qi, 1, True)
        update(1, qi, 1, True)

    a0 = acc_ref[0]
    a1 = acc_ref[1]
    l0 = pltpu.roll(a0, HEAD_DIM, axis=1)
    l1 = pltpu.roll(a1, HEAD_DIM, axis=1)
    o_ref[...] = jnp.where(lo, a0 / l0, a1 / l1).astype(BF16)


def _attn(q, kb, vb, ct, tq):
    B, L, _ = q.shape
    nq = L // tq
    npair = ATT_HEADS // 2
    return pl.pallas_call(
        functools.partial(_attn_kernel, tq=tq),
        grid=(B, npair, nq),
        in_specs=[
            pl.BlockSpec((None, tq, LANES), lambda b, hp, qi: (b, qi, hp)),
            pl.BlockSpec((None, L, LANES), lambda b, hp, qi: (b, 0, hp)),
            pl.BlockSpec((None, L, LANES), lambda b, hp, qi: (b, 0, hp)),
            pl.BlockSpec((None, None, 2, L), lambda b, hp, qi: (b, hp, 0, 0)),
        ],
        out_specs=pl.BlockSpec((None, tq, LANES), lambda b, hp, qi: (b, qi, hp)),
        out_shape=jax.ShapeDtypeStruct((B, L, ATT_WIDTH), BF16),
        scratch_shapes=[pltpu.VMEM((2, tq, LANES), BF16), pltpu.VMEM((2, 2, tq, tq), F32),
                        pltpu.VMEM((2, tq, LANES), F32), pltpu.VMEM((2, tq, LANES), F32)],
        compiler_params=_params("arbitrary", "arbitrary", "arbitrary"),
        name="attn_prompt",
    )(q, kb, vb, ct)


def _split3_dot(a, b):
    a1 = a.astype(BF16)
    r1 = a - a1.astype(F32)
    a2 = r1.astype(BF16)
    a3 = (r1 - a2.astype(F32)).astype(BF16)
    return _dot(a1, b) + _dot(a2, b) + _dot(a3, b)


def _attn_sample_kernel(pt_ref, q_ref, kn_ref, vn_ref, lfn_ref, u_ref, e_ref, *rest, n_pages, page, t_new):
    del pt_ref
    k_refs = rest[:n_pages]
    v_refs = rest[n_pages:2 * n_pages]
    lf_refs = rest[2 * n_pages:3 * n_pages]
    o_ref = rest[3 * n_pages]
    nh = ATT_HEADS
    nrow = t_new * nh
    nkey = page * nh

    lf = jnp.concatenate([r[...] for r in lf_refs], axis=0)
    local = _split3_dot(lf, u_ref[...])
    carry = jnp.zeros((nh, 1), F32)
    cum = []
    for j in range(n_pages):
        cj = local[j * nh:(j + 1) * nh, :] + carry
        carry = cj[:, page - 1:page]
        cum.append(cj)
    cum = jnp.concatenate(cum, axis=0) * LOG2E
    bias = _split3_dot(cum, e_ref[...])

    q = q_ref[...]
    row_h = jnp.bitwise_and(lax.broadcasted_iota(jnp.int32, (nrow, nkey), 0), nh - 1)
    lane_h = jnp.bitwise_and(lax.broadcasted_iota(jnp.int32, (nrow, nkey), 1), nh - 1)
    same_head = row_h == lane_h
    s_pages = []
    for j in range(n_pages):
        s = _dot_nt(q, k_refs[j][...].reshape(nkey, HEAD_DIM).astype(BF16))
        bj = jnp.concatenate([bias[j * nh:(j + 1) * nh, :]] * t_new, axis=0)
        s_pages.append(jnp.where(same_head, s - bj, NEG_INF))

    s_n = _dot_nt(q, kn_ref[...].astype(BF16))
    lfn = lfn_ref[...]
    lane_t = jnp.right_shift(lax.broadcasted_iota(jnp.int32, (nh, nrow), 1), _log2(nh))
    bias_n = jnp.zeros((nh, nrow), F32)
    for t in range(t_new):
        carry = carry + lfn[:, t:t + 1]
        bias_n = jnp.where(lane_t == t, carry * LOG2E, bias_n)
    bias_n = jnp.concatenate([bias_n] * t_new, axis=0)
    r_i = lax.broadcasted_iota(jnp.int32, (nrow, nrow), 0)
    c_i = lax.broadcasted_iota(jnp.int32, (nrow, nrow), 1)
    visible = ((jnp.bitwise_and(r_i, nh - 1) == jnp.bitwise_and(c_i, nh - 1))
               & (jnp.right_shift(c_i, _log2(nh)) <= jnp.right_shift(r_i, _log2(nh))))
    s_n = jnp.where(visible, s_n - bias_n, NEG_INF)

    m = jnp.max(s_n, axis=1, keepdims=True)
    for s in s_pages:
        m = jnp.maximum(m, jnp.max(s, axis=1, keepdims=True))
    p_n = jnp.exp2(s_n - m)
    l = jnp.sum(p_n, axis=1, keepdims=True)
    acc = _dot(p_n.astype(BF16), vn_ref[...].astype(BF16))
    for j in range(n_pages):
        p = jnp.exp2(s_pages[j] - m)
        l = l + jnp.sum(p, axis=1, keepdims=True)
        acc = acc + _dot(p.astype(BF16), v_refs[j][...].reshape(nkey, HEAD_DIM).astype(BF16))
    o_ref[...] = (acc / l).astype(BF16)


def _attn_sample(page_table, q, kn, vn, lfn, upper, stretch, cache_k, cache_v, cache_lft, layer):
    Bd, nrow, _ = q.shape
    n_pages = page_table.shape[1]
    page = cache_lft.shape[3]

    def page_spec(shape, j):
        return pl.BlockSpec((None, None) + shape, lambda b, pt: (layer, pt[b, j]) + (0,) * len(shape))

    seq = lambda shape: pl.BlockSpec((None,) + shape, lambda b, pt: (b, 0, 0))
    const = lambda a: pl.BlockSpec(a.shape, lambda b, pt: (0, 0), pipeline_mode=pl.Buffered(1))
    in_specs = [seq((nrow, HEAD_DIM))] * 3 + [seq((ATT_HEADS, LANES)), const(upper), const(stretch)]
    in_specs += [page_spec((page, ATT_HEADS, HEAD_DIM), j) for j in range(n_pages)]
    in_specs += [page_spec((page, ATT_HEADS, HEAD_DIM), j) for j in range(n_pages)]
    in_specs += [page_spec((ATT_HEADS, page), j) for j in range(n_pages)]
    grid_spec = pltpu.PrefetchScalarGridSpec(
        num_scalar_prefetch=1, grid=(Bd,), in_specs=in_specs,
        out_specs=pl.BlockSpec((None, nrow, HEAD_DIM), lambda b, pt: (b, 0, 0)))
    return pl.pallas_call(
        functools.partial(_attn_sample_kernel, n_pages=n_pages, page=page, t_new=nrow // ATT_HEADS),
        grid_spec=grid_spec,
        out_shape=jax.ShapeDtypeStruct((Bd, nrow, HEAD_DIM), BF16),
        compiler_params=_params("arbitrary"),
        name="attn_sample",
    )(page_table, q, kn, vn, lfn, upper, stretch,
      *([cache_k] * n_pages), *([cache_v] * n_pages), *([cache_lft] * n_pages))


def _s5_kernel(u_ref, h0_ref, lre_ref, lim_ref, ldt_ref, bblk_ref, cblk_ref, d_ref,
               w1_ref, b1_ref, w2_ref, b2_ref, y_ref, hfin_ref,
               x_scr, h_scr, st_scr, end_scr, *, R, Tt, two_pass):
    p = pl.program_id(0)
    i = pl.program_id(1)
    nt = pl.num_programs(1)
    n = S5_N
    last_pass = 1 if two_pass else 0

    dt = jnp.exp(ldt_ref[...])
    lr = lre_ref[...]
    li = lim_ref[...]
    mag = jnp.exp(lr * dt)
    a_re = mag * jnp.cos(li * dt)
    a_im = mag * jnp.sin(li * dt)
    den = lr * lr + li * li
    z_re = ((a_re - 1.0) * lr + a_im * li) / den
    z_im = (a_im * lr - (a_re - 1.0) * li) / den

    u = u_ref[...]
    bu = _dot(u.astype(BF16), bblk_ref[...])
    bur = bu[:, :n]
    bui = bu[:, n:]
    x_scr[:, :n] = z_re * bur - z_im * bui
    x_scr[:, n:] = z_re * bui + z_im * bur

    @pl.when((i == 0) & (p == 0))
    def _():
        st_scr[...] = h0_ref[...]

    if two_pass:
        @pl.when((i == 0) & (p == 1))
        def _():
            rolled = pltpu.roll(end_scr[...], R // 2, axis=0)
            first = lax.broadcasted_iota(jnp.int32, (R, 2 * n), 0) < R // 2
            st_scr[...] = jnp.where(first, h0_ref[...], rolled)

    ar = jnp.broadcast_to(a_re, (R, n))
    ai = jnp.broadcast_to(a_im, (R, n))

    def scan(store):
        def step(t, carry):
            hr, hi = carry
            r0 = t * R if isinstance(t, int) else pl.multiple_of(t * R, R)
            xr = x_scr[pl.ds(r0, R), :n]
            xi = x_scr[pl.ds(r0, R), n:]
            nr = ar * hr - ai * hi + xr
            ni = ar * hi + ai * hr + xi
            if store:
                h_scr[pl.ds(r0, R), :n] = nr
                h_scr[pl.ds(r0, R), n:] = ni
            return nr, ni

        if R == SUBLANES:
            hr, hi = lax.fori_loop(0, Tt, step, (st_scr[:, :n], st_scr[:, n:]), unroll=2)
        else:
            hr, hi = st_scr[:, :n], st_scr[:, n:]
            for t in range(Tt):
                hr, hi = step(t, (hr, hi))
        st_scr[:, :n] = hr
        st_scr[:, n:] = hi

    if two_pass:
        @pl.when(p == 0)
        def _():
            scan(False)

            @pl.when(i == nt - 1)
            def _():
                end_scr[...] = st_scr[...]

    @pl.when(p == last_pass)
    def _():
        scan(True)
        y = _dot(h_scr[...].astype(BF16), cblk_ref[...]) + d_ref[...] * u
        z = _gelu(y).astype(BF16)
        out = (_dot(z, w1_ref[...]) + b1_ref[...]) * _sigmoid(_dot(z, w2_ref[...]) + b2_ref[...])
        y_ref[...] = out.astype(BF16)

        @pl.when(i == nt - 1)
        def _():
            hfin_ref[...] = st_scr[...]


def _s5(u, h0, lre, lim, ldt, bblk, cblk, d, w1, b1, w2, b2, *, R, Tt, two_pass):
    rows = u.shape[0]
    tile = R * Tt
    nt = rows // tile
    npass = 2 if two_pass else 1
    n2 = 2 * S5_N
    consts = (h0, lre, lim, ldt, bblk, cblk, d, w1, b1, w2, b2)
    return pl.pallas_call(
        functools.partial(_s5_kernel, R=R, Tt=Tt, two_pass=two_pass),
        grid=(npass, nt),
        in_specs=[pl.BlockSpec((tile, S5_WIDTH), lambda p, i: (i, 0))] + [_resident(a.shape) for a in consts],
        out_specs=(pl.BlockSpec((tile, S5_WIDTH), lambda p, i: (i * p if two_pass else i, 0)),
                   pl.BlockSpec((R, n2), lambda p, i: (0, 0))),
        out_shape=(jax.ShapeDtypeStruct((rows, S5_WIDTH), BF16), jax.ShapeDtypeStruct((R, n2), F32)),
        scratch_shapes=[pltpu.VMEM((tile, n2), F32), pltpu.VMEM((tile, n2), F32),
                        pltpu.VMEM((R, n2), F32), pltpu.VMEM((R, n2), F32)],
        compiler_params=_params("arbitrary", "arbitrary"),
        name="s5",
    )(u, *consts)


def _pool_kernel(p_ref, halo_ref, w_ref, sc_ref, y_ref, ext, a2, a4, a8, a16, *, R, tm, start_pos, nt):
    i = pl.program_id(1)
    H = POOL_HALO * R
    n = H + tm

    @pl.when(i == 0)
    def _():
        ext[0:H, :] = halo_ref[...]

    x = p_ref[...]
    ext[H:n, :] = x
    n2, n4, n8, n16 = n - R, n - 3 * R, n - 7 * R, n - 15 * R
    a2[0:n2, :] = ext[R:R + n2, :] + ext[0:n2, :]
    a4[0:n4, :] = a2[2 * R:2 * R + n4, :] + a2[0:n4, :]
    a8[0:n8, :] = a4[4 * R:4 * R + n8, :] + a4[0:n8, :]
    a16[0:n16, :] = a8[8 * R:8 * R + n16, :] + a8[0:n16, :]
    s2 = a2[15 * R:15 * R + tm, :]
    s4 = a4[13 * R:13 * R + tm, :]
    s8 = a8[9 * R:9 * R + tm, :]
    s16 = a16[R:R + tm, :]
    pos = start_pos + jnp.right_shift(i * tm + lax.broadcasted_iota(jnp.int32, (tm, POOL_WIDTH), 0), _log2(R))
    lane = lax.broadcasted_iota(jnp.int32, (tm, POOL_WIDTH), 1)
    cnt = lambda w: jnp.minimum(w, pos + 1).astype(F32)
    mean = jnp.where(lane < POOL_GW, s2 / cnt(2),
                     jnp.where(lane < 2 * POOL_GW, s4 / cnt(4),
                               jnp.where(lane < 3 * POOL_GW, s8 / cnt(8), s16 / cnt(16))))
    m = (mean - x).astype(BF16)
    y_ref[...] = (_dot(m, w_ref[...]) * sc_ref[...]).astype(BF16)
    if nt > 1:
        ext[0:H, :] = ext[tm:tm + H, :]


def _pool(p, halo, wblk, scale, *, R, tm, start_pos):
    G, L, W = p.shape
    nt = L // tm
    H = POOL_HALO * R
    buf = pltpu.VMEM((H + tm, W), F32)
    return pl.pallas_call(
        functools.partial(_pool_kernel, R=R, tm=tm, start_pos=start_pos, nt=nt),
        grid=(G, nt),
        in_specs=[pl.BlockSpec((None, tm, W), lambda g, i: (g, i, 0)),
                  pl.BlockSpec((None, H, W), lambda g, i: (g, 0, 0)),
                  _resident(wblk.shape), _resident(scale.shape)],
        out_specs=pl.BlockSpec((None, tm, W), lambda g, i: (g, i, 0)),
        out_shape=jax.ShapeDtypeStruct((G, L, W), BF16),
        scratch_shapes=[buf] * 5,
        compiler_params=_params("arbitrary", "arbitrary"),
        name="pool",
    )(p, halo, wblk, scale)


def _merge_kernel(x_ref, ya_ref, ys_ref, yp_ref, g1_ref, wg_ref, wa_ref, wb_ref, wc_ref, wo_ref, g2_ref,
                  x1_ref, h2_ref, *, sub):
    d = x_ref.shape[-1]
    for r in range(x_ref.shape[0] // sub):
        rows = pl.ds(r * sub, sub)
        x = x_ref[rows, :]
        h = _rms(x, g1_ref[...]).astype(BF16)
        merged = (_sigmoid(_dot(h, wg_ref[:, :d])) * _dot(ya_ref[rows, :], wa_ref[...])
                  + _sigmoid(_dot(h, wg_ref[:, d:2 * d])) * _dot(ys_ref[rows, :], wb_ref[...])
                  + _sigmoid(_dot(h, wg_ref[:, 2 * d:])) * _dot(yp_ref[rows, :], wc_ref[...]))
        x1 = x + _dot(merged.astype(BF16), wo_ref[...])
        x1_ref[rows, :] = x1
        h2_ref[rows, :] = _rms(x1, g2_ref[...]).astype(BF16)


def _merge(x, ya, ys, yp, g1, wg, wa, wb, wc, wo, g2, tm):
    N, D = x.shape
    row = lambda w: pl.BlockSpec((tm, w), lambda i: (i, 0))
    consts = (g1, wg, wa, wb, wc, wo, g2)
    return pl.pallas_call(
        functools.partial(_merge_kernel, sub=min(tm, 256)),
        grid=(N // tm,),
        in_specs=[row(D), row(ATT_WIDTH), row(S5_WIDTH), row(POOL_WIDTH)] + [_resident(a.shape) for a in consts],
        out_specs=(row(D), row(D)),
        out_shape=(jax.ShapeDtypeStruct((N, D), F32), jax.ShapeDtypeStruct((N, D), BF16)),
        compiler_params=_params("arbitrary"),
        name="merge",
    )(x, ya, ys, yp, *consts)


def _ffn_kernel(h2_ref, x1_ref, halo_ref, wup_ref, cw_ref, cb_ref, wdn_ref, gfin_ref,
                x2_ref, tail_ref, ext, tail, act, *, R, tm, sub, H, fc, final_norm):
    i = pl.program_id(1)
    nt = pl.num_programs(1)
    dff = wdn_ref.shape[0]
    nc = dff // fc
    ns = tm // sub

    @pl.when(i == 0)
    def _():
        tail[...] = halo_ref[...]

    offsets = lambda c: (c * fc, dff + c * fc)

    def up(s, c):
        h2 = h2_ref[pl.ds(s * sub, sub), :]
        return [_dot(h2, wup_ref[:, off:off + fc]) for off in offsets(c)]

    def gate(s, c, us):
        halves = []
        for k, off in enumerate(offsets(c)):
            u = us[k]
            ext[k, 0:H, :] = tail[:, off:off + fc]
            ext[k, H:H + sub, :] = u
            cw = cw_ref[:, off:off + fc]
            cv = (cb_ref[:, off:off + fc] + cw[0:1, :] * ext[k, H - 2 * R:H - 2 * R + sub, :]
                  + cw[1:2, :] * ext[k, H - R:H - R + sub, :] + cw[2:3, :] * u)
            tail[:, off:off + fc] = ext[k, sub:sub + H, :]
            halves.append(cv)
        act[s, :, c * fc:(c + 1) * fc] = (_gelu(halves[0]) * halves[1]).astype(BF16)

    def down(s):
        rows = pl.ds(s * sub, sub)
        out = x1_ref[rows, :] + _dot(act[s], wdn_ref[...])
        if final_norm:
            out = _rms(out, gfin_ref[...])
        x2_ref[rows, :] = out

    cur = up(0, 0)
    for s in range(ns):
        for c in range(nc):
            if c + 1 < nc:
                nxt = up(s, c + 1)
            else:
                nxt = up(s + 1, 0) if s + 1 < ns else None
            gate(s, c, cur)
            cur = nxt
        down(s)

    @pl.when(i == nt - 1)
    def _():
        tail_ref[...] = tail[...]


def _ffn(h2, x1, halo, wup, cw, cb, wdn, gfin, *, R, tm, final_norm):
    G, L, D = x1.shape
    F2 = wup.shape[1]
    H = halo.shape[1]
    fc = 256
    sub = max(H, min(tm, 256))
    dff = wdn.shape[0]
    consts = (wup, cw, cb, wdn, gfin)
    return pl.pallas_call(
        functools.partial(_ffn_kernel, R=R, tm=tm, sub=sub, H=H, fc=fc, final_norm=final_norm),
        grid=(G, L // tm),
        in_specs=[pl.BlockSpec((None, tm, D), lambda g, i: (g, i, 0)),
                  pl.BlockSpec((None, tm, D), lambda g, i: (g, i, 0)),
                  pl.BlockSpec((None, H, F2), lambda g, i: (g, 0, 0))] + [_resident(a.shape) for a in consts],
        out_specs=(pl.BlockSpec((None, tm, D), lambda g, i: (g, i, 0)),
                   pl.BlockSpec((None, H, F2), lambda g, i: (g, 0, 0))),
        out_shape=(jax.ShapeDtypeStruct((G, L, D), F32), jax.ShapeDtypeStruct((G, H, F2), F32)),
        scratch_shapes=[pltpu.VMEM((2, H + sub, fc), F32), pltpu.VMEM((H, F2), F32),
                        pltpu.VMEM((tm // sub, sub, dff), BF16)],
        compiler_params=_params("arbitrary", "arbitrary"),
        name="ffn",
    )(h2, x1, halo, *consts)


def _blockdiag(w):
    G, a, b = w.shape
    eye = jnp.eye(G, dtype=w.dtype)
    return (w[:, :, None, :] * eye[:, None, :, None]).reshape(G * a, G * b)


def _layer_weights(l, w):
    d = w["w_in"].shape[1]
    o_f = 3 * ATT_WIDTH
    o_s = o_f + ATT_HEADS
    o_g = o_s + S5_WIDTH + POOL_WIDTH
    w_in = w["w_in"][l]
    row = lambda a: a.reshape(1, -1).astype(F32)
    bblk = jnp.concatenate([_blockdiag(jnp.swapaxes(w["s5_b_re"][l], 1, 2)),
                            _blockdiag(jnp.swapaxes(w["s5_b_im"][l], 1, 2))], axis=1)
    cblk = jnp.concatenate([_blockdiag(jnp.swapaxes(w["s5_c_re"][l], 1, 2)),
                            -_blockdiag(jnp.swapaxes(w["s5_c_im"][l], 1, 2))], axis=0)
    return dict(
        g1=row(w["norm_mix_g"][l]),
        wqkv=w_in[:, :o_f].astype(BF16),
        wf=jnp.pad(w_in[:, o_f:o_s], ((0, 0), (0, LANES - ATT_HEADS))).astype(BF16),
        wsp=w_in[:, o_s:o_g].astype(BF16),
        wg=w_in[:, o_g:].astype(BF16),
        bf=jnp.pad(row(w["b_f"][l]), ((0, 0), (0, LANES - ATT_HEADS))),
        lre=row(w["s5_lam_re"][l]), lim=row(w["s5_lam_im"][l]),
        ldt=row(jnp.broadcast_to(w["s5_log_dt"][l][:, None], (S5_GROUPS, S5_STATE))),
        bblk=bblk.astype(BF16), cblk=cblk.astype(BF16), d=row(w["s5_d"][l]),
        w1=w["s5_glu_w1"][l].astype(BF16), b1=row(w["s5_glu_b1"][l]),
        w2=w["s5_glu_w2"][l].astype(BF16), b2=row(w["s5_glu_b2"][l]),
        pw=_blockdiag(w["pool_w"][l]).astype(BF16), ps=row(w["pool_scale"][l]),
        wa=w["w_br_a"][l].astype(BF16), wb=w["w_br_b"][l].astype(BF16), wc=w["w_br_c"][l].astype(BF16),
        wo=w["w_out"][l].astype(BF16), g2=row(w["norm_ffn_g"][l]),
        wup=w["w_up"][l].astype(BF16), cw=w["conv_w"][l].astype(F32), cb=row(w["conv_b"][l]),
        wdn=w["w_down"][l].astype(BF16),
    )


def _tile(n, pref):
    return pref if n % pref == 0 else n


def kernel(x_prompt, x_sample, cache_k, cache_v, cache_logf, state_ssm_re, state_ssm_im, state_pool,
           state_ffn_conv, page_table, norm_mix_g, w_in, b_f, s5_lam_re, s5_lam_im, s5_log_dt, s5_b_re, s5_b_im,
           s5_c_re, s5_c_im, s5_d, s5_glu_w1, s5_glu_b1, s5_glu_w2, s5_glu_b2, pool_w, pool_scale, w_br_a,
           w_br_b, w_br_c, w_out, norm_ffn_g, w_up, conv_w, conv_b, w_down, norm_final_g):
    w = dict(norm_mix_g=norm_mix_g, w_in=w_in, b_f=b_f, s5_lam_re=s5_lam_re, s5_lam_im=s5_lam_im,
             s5_log_dt=s5_log_dt, s5_b_re=s5_b_re, s5_b_im=s5_b_im, s5_c_re=s5_c_re, s5_c_im=s5_c_im, s5_d=s5_d,
             s5_glu_w1=s5_glu_w1, s5_glu_b1=s5_glu_b1, s5_glu_w2=s5_glu_w2, s5_glu_b2=s5_glu_b2, pool_w=pool_w,
             pool_scale=pool_scale, w_br_a=w_br_a, w_br_b=w_br_b, w_br_c=w_br_c, w_out=w_out,
             norm_ffn_g=norm_ffn_g, w_up=w_up, conv_w=conv_w, conv_b=conv_b, w_down=w_down)
    depth = w_in.shape[0]
    B, L, D = x_prompt.shape
    Bd, T, _ = x_sample.shape
    F2 = w_up.shape[2]
    n_pool, page = cache_k.shape[1], cache_k.shape[2]
    n2 = 2 * S5_N
    gfin = norm_final_g.reshape(1, D).astype(F32)

    tm = _tile(L, 512)
    halves = 2
    Lh = L // halves
    Rp = halves * B
    assert Rp == SUBLANES, "prompt S5 scan keeps one (half, batch) group per vreg"
    Tt = _tile(Lh, 64)
    Ns = Bd * T

    ck, cv = cache_k, cache_v
    clft = jnp.swapaxes(cache_logf, 2, 3)
    pos_r = lax.broadcasted_iota(jnp.int32, (page, page), 0)
    pos_c = lax.broadcasted_iota(jnp.int32, (page, page), 1)
    upper = (pos_r <= pos_c).astype(BF16)
    key_pos = lax.broadcasted_iota(jnp.int32, (page, page * ATT_HEADS), 1) // ATT_HEADS
    stretch = (lax.broadcasted_iota(jnp.int32, (page, page * ATT_HEADS), 0) == key_pos).astype(BF16)
    pool_halo_s = jnp.pad(jnp.swapaxes(state_pool, 1, 2), ((0, 0), (1, 0), (0, 0), (0, 0)))
    pool_halo_s = pool_halo_s.reshape(depth, 1, POOL_HALO * Bd, POOL_WIDTH)
    conv_halo_s = jnp.swapaxes(state_ffn_conv, 1, 2).reshape(depth, 1, (CONV_TAPS - 1) * Bd, F2)
    h0_s = jnp.concatenate([state_ssm_re.reshape(depth, Bd, S5_N), state_ssm_im.reshape(depth, Bd, S5_N)], axis=-1)
    h0_p = jnp.zeros((Rp, n2), F32)
    pool_halo_p = jnp.zeros((B, POOL_HALO, POOL_WIDTH), F32)
    conv_halo_p = jnp.zeros((B, SUBLANES, F2), F32)

    xp = x_prompt
    xs = jnp.swapaxes(x_sample, 0, 1).reshape(1, Ns, D)
    st_p, st_s = [], []
    for l in range(depth):
        lw = _layer_weights(l, w)
        last = l == depth - 1

        q, k, v, kb, vb, lf, c, us, up = _proj(xp, lw["g1"], lw["wqkv"], lw["wf"], lw["wsp"], lw["bf"], tm, True)
        ct = jnp.swapaxes(c, 1, 2).reshape(B, ATT_HEADS // 2, 2, L)
        ya = _attn(q, kb, vb, ct, tm)
        u_perm = us.reshape(B, halves, Lh, S5_WIDTH).transpose(2, 1, 0, 3).reshape(Lh * Rp, S5_WIDTH)
        ys_perm, hfin = _s5(u_perm, h0_p, lw["lre"], lw["lim"], lw["ldt"], lw["bblk"], lw["cblk"], lw["d"],
                            lw["w1"], lw["b1"], lw["w2"], lw["b2"], R=Rp, Tt=Tt, two_pass=True)
        ys = ys_perm.reshape(Lh, halves, B, S5_WIDTH).transpose(2, 1, 0, 3).reshape(B * L, S5_WIDTH)
        yp = _pool(up, pool_halo_p, lw["pw"], lw["ps"], R=1, tm=tm, start_pos=0)
        x1, h2 = _merge(xp.reshape(B * L, D), ya.reshape(B * L, ATT_WIDTH), ys, yp.reshape(B * L, POOL_WIDTH),
                        lw["g1"], lw["wg"], lw["wa"], lw["wb"], lw["wc"], lw["wo"], lw["g2"], tm)
        xp, tail_p = _ffn(h2.reshape(B, L, D), x1.reshape(B, L, D), conv_halo_p, lw["wup"], lw["cw"], lw["cb"],
                          lw["wdn"], gfin, R=1, tm=tm, final_norm=last)
        st_p.append((k.reshape(B, L, ATT_HEADS, HEAD_DIM), v.reshape(B, L, ATT_HEADS, HEAD_DIM), lf,
                     hfin[B:, :S5_N].reshape(B, S5_GROUPS, S5_STATE), hfin[B:, S5_N:].reshape(B, S5_GROUPS, S5_STATE),
                     up[:, L - POOL_BUF:, :], tail_p[:, SUBLANES - (CONV_TAPS - 1):, :]))

        q, k, v, _, _, lf, _, us, up = _proj(xs, lw["g1"], lw["wqkv"], lw["wf"], lw["wsp"], lw["bf"], Ns, False)
        bt = lambda a: jnp.swapaxes(a.reshape(T, Bd, -1), 0, 1)
        k_b, v_b, lf_b = bt(k), bt(v), bt(lf)
        lfn = jnp.pad(jnp.swapaxes(lf_b, 1, 2), ((0, 0), (0, 0), (0, LANES - T)))
        th = lambda a: a.reshape(Bd, T * ATT_HEADS, HEAD_DIM)
        ya = _attn_sample(page_table, th(bt(q)), th(k_b), th(v_b), lfn, upper, stretch, ck, cv, clft, l)
        ya = jnp.swapaxes(ya.reshape(Bd, T, ATT_WIDTH), 0, 1).reshape(Ns, ATT_WIDTH)
        ys, hfin = _s5(us.reshape(Ns, S5_WIDTH), h0_s[l], lw["lre"], lw["lim"], lw["ldt"], lw["bblk"], lw["cblk"],
                       lw["d"], lw["w1"], lw["b1"], lw["w2"], lw["b2"], R=Bd, Tt=T, two_pass=False)
        yp = _pool(up, pool_halo_s[l], lw["pw"], lw["ps"], R=Bd, tm=Ns, start_pos=page_table.shape[1] * page)
        x1, h2 = _merge(xs.reshape(Ns, D), ya, ys, yp.reshape(Ns, POOL_WIDTH),
                        lw["g1"], lw["wg"], lw["wa"], lw["wb"], lw["wc"], lw["wo"], lw["g2"], Ns)
        xs, tail_s = _ffn(h2.reshape(1, Ns, D), x1.reshape(1, Ns, D), conv_halo_s[l], lw["wup"], lw["cw"], lw["cb"],
                          lw["wdn"], gfin, R=Bd, tm=Ns, final_norm=last)
        new_pool = jnp.concatenate([state_pool[l][:, T:, :], bt(up)], axis=1)
        new_conv = jnp.swapaxes(tail_s.reshape(CONV_TAPS - 1, Bd, F2), 0, 1)
        st_s.append((k_b.reshape(Bd, T, ATT_HEADS, HEAD_DIM), v_b.reshape(Bd, T, ATT_HEADS, HEAD_DIM), lf_b,
                     hfin[:, :S5_N].reshape(Bd, S5_GROUPS, S5_STATE), hfin[:, S5_N:].reshape(Bd, S5_GROUPS, S5_STATE),
                     new_pool, new_conv))

    y_prompt = xp
    y_sample = jnp.swapaxes(xs.reshape(T, Bd, D), 0, 1)
    outs_p = [jnp.stack(z) for z in zip(*st_p)]
    outs_s = [jnp.stack(z) for z in zip(*st_s)]
    return (y_prompt, y_sample, *outs_p, *outs_s)
```

```python
import functools

import jax
import jax.numpy as jnp
from jax import lax
from jax.experimental import pallas as pl
from jax.experimental.pallas import tpu as pltpu

F32 = jnp.float32
BF16 = jnp.bfloat16

ATT_HEADS = 8
HEAD_DIM = 64
ATT_WIDTH = ATT_HEADS * HEAD_DIM
S5_WIDTH = 256
S5_GROUPS = 16
S5_STATE = 64
S5_N = S5_GROUPS * S5_STATE
POOL_WIDTH = 256
POOL_WINDOWS = (2, 4, 8, 16)
POOL_GW = 64
POOL_BUF = 15
POOL_HALO = 16
CONV_TAPS = 3
RMS_EPS = 1e-6
NEG_INF = -1e30
LANES = 128
SUBLANES = 8
VMEM_LIMIT = 56 * 1024 * 1024
HIGHEST = lax.Precision.HIGHEST
LOG2E = 1.4426950408889634


def _params(*sem):
    return pltpu.CompilerParams(dimension_semantics=sem, vmem_limit_bytes=VMEM_LIMIT)


def _resident(shape):
    nd = len(shape)
    return pl.BlockSpec(shape, lambda *_: (0,) * nd, pipeline_mode=pl.Buffered(1))


def _rms(x, g):
    return x * lax.rsqrt(jnp.mean(x * x, axis=-1, keepdims=True) + RMS_EPS) * g


def _sigmoid(x):
    return 1.0 / (1.0 + jnp.exp(-x))


def _gelu(x):
    return 0.5 * x * (1.0 + jnp.tanh(0.7978845608028654 * (x + 0.044715 * (x * x * x))))


def _log2(n):
    assert n & (n - 1) == 0, n
    return n.bit_length() - 1


def _dot(a, b):
    return jnp.dot(a, b, preferred_element_type=F32)


def _dot_nt(a, b):
    return lax.dot_general(a, b, (((1,), (1,)), ((), ())), preferred_element_type=F32)


def _proj_kernel(x_ref, g_ref, wqkv_ref, wf_ref, wsp_ref, bf_ref,
                 q_ref, k_ref, v_ref, kb_ref, vb_ref, lf_ref, c_ref, us_ref, up_ref,
                 carry_ref, *, tm, sub, with_cumsum):
    i = pl.program_id(1)

    @pl.when(i == 0)
    def _():
        carry_ref[...] = jnp.zeros_like(carry_ref)

    row = lax.broadcasted_iota(jnp.int32, (sub, sub), 0)
    col = lax.broadcasted_iota(jnp.int32, (sub, sub), 1)
    tri = (row >= col).astype(F32)
    carry = carry_ref[...]
    for r in range(tm // sub):
        rows = pl.ds(r * sub, sub)
        h = _rms(x_ref[rows, :], g_ref[...]).astype(BF16)
        qkv = _dot(h, wqkv_ref[...])
        q_ref[rows, :] = (qkv[:, :ATT_WIDTH] * (LOG2E * HEAD_DIM ** -0.5)).astype(BF16)
        k = qkv[:, ATT_WIDTH:2 * ATT_WIDTH]
        v = qkv[:, 2 * ATT_WIDTH:]
        k_ref[rows, :] = k
        v_ref[rows, :] = v
        kb_ref[rows, :] = k.astype(BF16)
        vb_ref[rows, :] = v.astype(BF16)
        f = _dot(h, wf_ref[...]) + bf_ref[...]
        lf = -(jnp.maximum(-f, 0.0) + jnp.log1p(jnp.exp(-jnp.abs(f))))
        lf_ref[rows, :] = lf[:, :ATT_HEADS]
        if with_cumsum:
            c = carry + jnp.dot(tri, lf, precision=HIGHEST, preferred_element_type=F32)
            carry = c[sub - 1:sub, :]
        else:
            c = lf
        c_ref[rows, :] = c[:, :ATT_HEADS] * LOG2E
        sp = _dot(h, wsp_ref[...])
        us_ref[rows, :] = sp[:, :S5_WIDTH]
        up_ref[rows, :] = sp[:, S5_WIDTH:]
    carry_ref[...] = carry


def _proj(x, g, wqkv, wf, wsp, bf, tm, with_cumsum):
    B, L, D = x.shape
    nt = L // tm
    row = lambda w: pl.BlockSpec((None, tm, w), lambda b, i: (b, i, 0))
    out_shape = (
        jax.ShapeDtypeStruct((B, L, ATT_WIDTH), BF16),
        jax.ShapeDtypeStruct((B, L, ATT_WIDTH), F32),
        jax.ShapeDtypeStruct((B, L, ATT_WIDTH), F32),
        jax.ShapeDtypeStruct((B, L, ATT_WIDTH), BF16),
        jax.ShapeDtypeStruct((B, L, ATT_WIDTH), BF16),
        jax.ShapeDtypeStruct((B, L, ATT_HEADS), F32),
        jax.ShapeDtypeStruct((B, L, ATT_HEADS), F32),
        jax.ShapeDtypeStruct((B, L, S5_WIDTH), F32),
        jax.ShapeDtypeStruct((B, L, POOL_WIDTH), F32),
    )
    return pl.pallas_call(
        functools.partial(_proj_kernel, tm=tm, sub=min(tm, 256), with_cumsum=with_cumsum),
        grid=(B, nt),
        in_specs=[row(D), _resident(g.shape), _resident(wqkv.shape), _resident(wf.shape),
                  _resident(wsp.shape), _resident(bf.shape)],
        out_specs=(row(ATT_WIDTH),) * 5 + (row(ATT_HEADS),) * 2 + (row(S5_WIDTH), row(POOL_WIDTH)),
        out_shape=out_shape,
        scratch_shapes=[pltpu.VMEM((1, LANES), F32)],
        compiler_params=_params("arbitrary", "arbitrary"),
        name="proj",
    )(x, g, wqkv, wf, wsp, bf)


def _attn_kernel(q_ref, k_ref, v_ref, c_ref, o_ref, qm_ref, s_ref, m_ref, acc_ref, *, tq):
    qi = pl.program_id(2)
    q2 = q_ref[...]
    lo = lax.broadcasted_iota(jnp.int32, (tq, LANES), 1) < HEAD_DIM
    zero = jnp.zeros_like(q2)
    qm_ref[0] = jnp.where(lo, q2, zero)
    qm_ref[1] = jnp.where(lo, zero, q2)
    m_ref[...] = jnp.full_like(m_ref, NEG_INF)
    acc_ref[...] = jnp.zeros_like(acc_ref)

    def scores(e, j, slot):
        start = pl.multiple_of(j * tq, tq)
        s_ref[slot, e] = _dot_nt(qm_ref[e], k_ref[pl.ds(start, tq), :]) - c_ref[e:e + 1, pl.ds(start, tq)]

    def update(e, j, slot, masked):
        s = s_ref[slot, e]
        if masked:
            row = lax.broadcasted_iota(jnp.int32, (tq, tq), 0)
            col = lax.broadcasted_iota(jnp.int32, (tq, tq), 1)
            s = jnp.where(col <= row, s, NEG_INF)
        m_old = m_ref[e]
        m_new = jnp.maximum(m_old, jnp.max(s, axis=1, keepdims=True))
        alpha = jnp.exp2(m_old - m_new)
        p = jnp.exp2(s - jnp.concatenate([m_new] * (tq // LANES), axis=1))
        start = pl.multiple_of(j * tq, tq)
        vj = v_ref[pl.ds(start, tq), :]
        one = jnp.ones_like(vj)
        vje = jnp.where(lo, vj, one) if e == 0 else jnp.where(lo, one, vj)
        acc_ref[e] = alpha * acc_ref[e] + _dot(p.astype(BF16), vje)
        m_ref[e] = m_new

    scores(0, 0, 0)
    scores(1, 0, 0)

    def step(j, slot):
        for e in range(2):
            scores(e, j + 1, 1 - slot)
            update(e, j, slot, False)

    def body(jj, carry):
        step(2 * jj, 0)
        step(2 * jj + 1, 1)
        return carry

    npair = jnp.right_shift(qi, 1)
    lax.fori_loop(0, npair, body, 0)

    @pl.when(qi == 2 * npair)
    def _():
        update(0, qi, 0, True)
        update(1, qi, 0, True)

    @pl.when(qi != 2 * npair)
    def _():
        step(qi - 1, 0)
        update(0, qi, 1, True)
        update(1, qi, 1, True)

    a0 = acc_ref[0]
    a1 = acc_ref[1]
    l0 = pltpu.roll(a0, HEAD_DIM, axis=1)
    l1 = pltpu.roll(a1, HEAD_DIM, axis=1)
    o_ref[...] = jnp.where(lo, a0 / l0, a1 / l1).astype(BF16)


def _attn(q, kb, vb, ct, tq):
    B, L, _ = q.shape
    nq = L // tq
    npair = ATT_HEADS // 2
    return pl.pallas_call(
        functools.partial(_attn_kernel, tq=tq),
        grid=(B, npair, nq),
        in_specs=[
            pl.BlockSpec((None, tq, LANES), lambda b, hp, qi: (b, qi, hp)),
            pl.BlockSpec((None, L, LANES), lambda b, hp, qi: (b, 0, hp)),
            pl.BlockSpec((None, L, LANES), lambda b, hp, qi: (b, 0, hp)),
            pl.BlockSpec((None, None, 2, L), lambda b, hp, qi: (b, hp, 0, 0)),
        ],
        out_specs=pl.BlockSpec((None, tq, LANES), lambda b, hp, qi: (b, qi, hp)),
        out_shape=jax.ShapeDtypeStruct((B, L, ATT_WIDTH), BF16),
        scratch_shapes=[pltpu.VMEM((2, tq, LANES), BF16), pltpu.VMEM((2, 2, tq, tq), F32),
                        pltpu.VMEM((2, tq, LANES), F32), pltpu.VMEM((2, tq, LANES), F32)],
        compiler_params=_params("arbitrary", "arbitrary", "arbitrary"),
        name="attn_prompt",
    )(q, kb, vb, ct)


def _split3_dot(a, b):
    a1 = a.astype(BF16)
    r1 = a - a1.astype(F32)
    a2 = r1.astype(BF16)
    a3 = (r1 - a2.astype(F32)).astype(BF16)
    return _dot(a1, b) + _dot(a2, b) + _dot(a3, b)


def _attn_sample_kernel(pt_ref, q_ref, kn_ref, vn_ref, lfn_ref, u_ref, *rest, n_pages, page, t_new):
    del pt_ref
    k_refs = rest[:n_pages]
    v_refs = rest[n_pages:2 * n_pages]
    lf_refs = rest[2 * n_pages:3 * n_pages]
    o_ref = rest[3 * n_pages]
    nh = ATT_HEADS
    nrow = t_new * nh
    sub = lax.broadcasted_iota(jnp.int32, (nh, ATT_WIDTH), 0)
    lane = lax.broadcasted_iota(jnp.int32, (nh, ATT_WIDTH), 1)
    head_mask = jnp.right_shift(lane, _log2(HEAD_DIM)) == sub
    q = q_ref[...].astype(F32)
    qexp = jnp.concatenate(
        [jnp.where(head_mask, jnp.broadcast_to(q[t:t + 1, :], (nh, ATT_WIDTH)), 0.0) for t in range(t_new)],
        axis=0)
    qexp_b = qexp.astype(BF16)

    lf = jnp.concatenate([r[...] for r in lf_refs], axis=0)
    local = _split3_dot(lf, u_ref[...])
    carry = jnp.zeros((nh, 1), F32)
    cum = []
    for j in range(n_pages):
        cj = local[j * nh:(j + 1) * nh, :] + carry
        carry = cj[:, page - 1:page]
        cum.append(cj)
    cum = jnp.concatenate(cum, axis=0) * LOG2E

    s_pages = []
    for j in range(n_pages):
        kt = k_refs[j][...].reshape(ATT_WIDTH, page).astype(BF16)
        bj = jnp.concatenate([cum[j * nh:(j + 1) * nh, :]] * t_new, axis=0)
        s_pages.append(_dot(qexp_b, kt) - bj)

    kn = kn_ref[...].astype(BF16).astype(F32)
    vn = vn_ref[...].astype(BF16).astype(F32)
    lfn = lfn_ref[...]
    tq_of_row = jnp.right_shift(lax.broadcasted_iota(jnp.int32, (nrow, 1), 0), _log2(nh))
    s_new = []
    for t in range(t_new):
        carry = carry + lfn[:, t:t + 1]
        st = (jnp.sum(qexp * kn[t:t + 1, :], axis=-1, keepdims=True)
              - jnp.concatenate([carry * LOG2E] * t_new, axis=0))
        s_new.append(jnp.where(tq_of_row >= t, st, NEG_INF))

    m = s_new[0]
    for st in s_new[1:]:
        m = jnp.maximum(m, st)
    for s in s_pages:
        m = jnp.maximum(m, jnp.max(s, axis=-1, keepdims=True))
    l = jnp.zeros((nrow, 1), F32)
    acc = jnp.zeros((nrow, ATT_WIDTH), F32)
    for j in range(n_pages):
        p = jnp.exp2(s_pages[j] - m)
        l = l + jnp.sum(p, axis=-1, keepdims=True)
        vt = v_refs[j][...].reshape(ATT_WIDTH, page).astype(BF16)
        acc = acc + _dot_nt(p.astype(BF16), vt)
    for t in range(t_new):
        p = jnp.exp2(s_new[t] - m)
        l = l + p
        acc = acc + p.astype(BF16).astype(F32) * vn[t:t + 1, :]
    out = acc / l
    rows = [jnp.sum(jnp.where(head_mask, out[t * nh:(t + 1) * nh, :], 0.0), axis=0, keepdims=True)
            for t in range(t_new)]
    o_ref[...] = jnp.concatenate(rows, axis=0).astype(BF16)


def _attn_sample(page_table, q, kn, vn, lfn, upper, cache_kt, cache_vt, cache_lft, layer):
    Bd, T, _ = q.shape
    n_pages = page_table.shape[1]
    page = cache_lft.shape[3]

    def page_spec(shape, j):
        return pl.BlockSpec((None, None) + shape, lambda b, pt: (layer, pt[b, j]) + (0,) * len(shape))

    seq = lambda shape: pl.BlockSpec((None,) + shape, lambda b, pt: (b, 0, 0))
    const = lambda a: pl.BlockSpec(a.shape, lambda b, pt: (0, 0), pipeline_mode=pl.Buffered(1))
    in_specs = [seq((T, ATT_WIDTH))] * 3 + [seq((ATT_HEADS, LANES)), const(upper)]
    in_specs += [page_spec((ATT_HEADS, HEAD_DIM, page), j) for j in range(n_pages)]
    in_specs += [page_spec((ATT_HEADS, HEAD_DIM, page), j) for j in range(n_pages)]
    in_specs += [page_spec((ATT_HEADS, page), j) for j in range(n_pages)]
    grid_spec = pltpu.PrefetchScalarGridSpec(
        num_scalar_prefetch=1, grid=(Bd,), in_specs=in_specs,
        out_specs=pl.BlockSpec((None, T, ATT_WIDTH), lambda b, pt: (b, 0, 0)))
    return pl.pallas_call(
        functools.partial(_attn_sample_kernel, n_pages=n_pages, page=page, t_new=T),
        grid_spec=grid_spec,
        out_shape=jax.ShapeDtypeStruct((Bd, T, ATT_WIDTH), BF16),
        compiler_params=_params("arbitrary"),
        name="attn_sample",
    )(page_table, q, kn, vn, lfn, upper,
      *([cache_kt] * n_pages), *([cache_vt] * n_pages), *([cache_lft] * n_pages))


def _s5_kernel(u_ref, h0_ref, lre_ref, lim_ref, ldt_ref, bblk_ref, cblk_ref, d_ref,
               w1_ref, b1_ref, w2_ref, b2_ref, y_ref, hfin_ref,
               x_scr, h_scr, st_scr, end_scr, *, R, Tt, two_pass):
    p = pl.program_id(0)
    i = pl.program_id(1)
    nt = pl.num_programs(1)
    n = S5_N
    last_pass = 1 if two_pass else 0

    dt = jnp.exp(ldt_ref[...])
    lr = lre_ref[...]
    li = lim_ref[...]
    mag = jnp.exp(lr * dt)
    a_re = mag * jnp.cos(li * dt)
    a_im = mag * jnp.sin(li * dt)
    den = lr * lr + li * li
    z_re = ((a_re - 1.0) * lr + a_im * li) / den
    z_im = (a_im * lr - (a_re - 1.0) * li) / den

    u = u_ref[...]
    bu = _dot(u.astype(BF16), bblk_ref[...])
    bur = bu[:, :n]
    bui = bu[:, n:]
    x_scr[:, :n] = z_re * bur - z_im * bui
    x_scr[:, n:] = z_re * bui + z_im * bur

    @pl.when((i == 0) & (p == 0))
    def _():
        st_scr[...] = h0_ref[...]

    if two_pass:
        @pl.when((i == 0) & (p == 1))
        def _():
            rolled = pltpu.roll(end_scr[...], R // 2, axis=0)
            first = lax.broadcasted_iota(jnp.int32, (R, 2 * n), 0) < R // 2
            st_scr[...] = jnp.where(first, h0_ref[...], rolled)

    ar = jnp.broadcast_to(a_re, (R, n))
    ai = jnp.broadcast_to(a_im, (R, n))

    def scan(store):
        def step(t, carry):
            hr, hi = carry
            r0 = t * R if isinstance(t, int) else pl.multiple_of(t * R, R)
            xr = x_scr[pl.ds(r0, R), :n]
            xi = x_scr[pl.ds(r0, R), n:]
            nr = ar * hr - ai * hi + xr
            ni = ar * hi + ai * hr + xi
            if store:
                h_scr[pl.ds(r0, R), :n] = nr
                h_scr[pl.ds(r0, R), n:] = ni
            return nr, ni

        if R == SUBLANES:
            hr, hi = lax.fori_loop(0, Tt, step, (st_scr[:, :n], st_scr[:, n:]), unroll=2)
        else:
            hr, hi = st_scr[:, :n], st_scr[:, n:]
            for t in range(Tt):
                hr, hi = step(t, (hr, hi))
        st_scr[:, :n] = hr
        st_scr[:, n:] = hi

    if two_pass:
        @pl.when(p == 0)
        def _():
            scan(False)

            @pl.when(i == nt - 1)
            def _():
                end_scr[...] = st_scr[...]

    @pl.when(p == last_pass)
    def _():
        scan(True)
        y = _dot(h_scr[...].astype(BF16), cblk_ref[...]) + d_ref[...] * u
        z = _gelu(y).astype(BF16)
        out = (_dot(z, w1_ref[...]) + b1_ref[...]) * _sigmoid(_dot(z, w2_ref[...]) + b2_ref[...])
        y_ref[...] = out.astype(BF16)

        @pl.when(i == nt - 1)
        def _():
            hfin_ref[...] = st_scr[...]


def _s5(u, h0, lre, lim, ldt, bblk, cblk, d, w1, b1, w2, b2, *, R, Tt, two_pass):
    rows = u.shape[0]
    tile = R * Tt
    nt = rows // tile
    npass = 2 if two_pass else 1
    n2 = 2 * S5_N
    consts = (h0, lre, lim, ldt, bblk, cblk, d, w1, b1, w2, b2)
    return pl.pallas_call(
        functools.partial(_s5_kernel, R=R, Tt=Tt, two_pass=two_pass),
        grid=(npass, nt),
        in_specs=[pl.BlockSpec((tile, S5_WIDTH), lambda p, i: (i, 0))] + [_resident(a.shape) for a in consts],
        out_specs=(pl.BlockSpec((tile, S5_WIDTH), lambda p, i: (i * p if two_pass else i, 0)),
                   pl.BlockSpec((R, n2), lambda p, i: (0, 0))),
        out_shape=(jax.ShapeDtypeStruct((rows, S5_WIDTH), BF16), jax.ShapeDtypeStruct((R, n2), F32)),
        scratch_shapes=[pltpu.VMEM((tile, n2), F32), pltpu.VMEM((tile, n2), F32),
                        pltpu.VMEM((R, n2), F32), pltpu.VMEM((R, n2), F32)],
        compiler_params=_params("arbitrary", "arbitrary"),
        name="s5",
    )(u, *consts)


def _pool_kernel(p_ref, halo_ref, w_ref, sc_ref, y_ref, ext, a2, a4, a8, a16, *, R, tm, start_pos, nt):
    i = pl.program_id(1)
    H = POOL_HALO * R
    n = H + tm

    @pl.when(i == 0)
    def _():
        ext[0:H, :] = halo_ref[...]

    x = p_ref[...]
    ext[H:n, :] = x
    n2, n4, n8, n16 = n - R, n - 3 * R, n - 7 * R, n - 15 * R
    a2[0:n2, :] = ext[R:R + n2, :] + ext[0:n2, :]
    a4[0:n4, :] = a2[2 * R:2 * R + n4, :] + a2[0:n4, :]
    a8[0:n8, :] = a4[4 * R:4 * R + n8, :] + a4[0:n8, :]
    a16[0:n16, :] = a8[8 * R:8 * R + n16, :] + a8[0:n16, :]
    s2 = a2[15 * R:15 * R + tm, :]
    s4 = a4[13 * R:13 * R + tm, :]
    s8 = a8[9 * R:9 * R + tm, :]
    s16 = a16[R:R + tm, :]
    pos = start_pos + jnp.right_shift(i * tm + lax.broadcasted_iota(jnp.int32, (tm, POOL_WIDTH), 0), _log2(R))
    lane = lax.broadcasted_iota(jnp.int32, (tm, POOL_WIDTH), 1)
    cnt = lambda w: jnp.minimum(w, pos + 1).astype(F32)
    mean = jnp.where(lane < POOL_GW, s2 / cnt(2),
                     jnp.where(lane < 2 * POOL_GW, s4 / cnt(4),
                               jnp.where(lane < 3 * POOL_GW, s8 / cnt(8), s16 / cnt(16))))
    m = (mean - x).astype(BF16)
    y_ref[...] = (_dot(m, w_ref[...]) * sc_ref[...]).astype(BF16)
    if nt > 1:
        ext[0:H, :] = ext[tm:tm + H, :]


def _pool(p, halo, wblk, scale, *, R, tm, start_pos):
    G, L, W = p.shape
    nt = L // tm
    H = POOL_HALO * R
    buf = pltpu.VMEM((H + tm, W), F32)
    return pl.pallas_call(
        functools.partial(_pool_kernel, R=R, tm=tm, start_pos=start_pos, nt=nt),
        grid=(G, nt),
        in_specs=[pl.BlockSpec((None, tm, W), lambda g, i: (g, i, 0)),
                  pl.BlockSpec((None, H, W), lambda g, i: (g, 0, 0)),
                  _resident(wblk.shape), _resident(scale.shape)],
        out_specs=pl.BlockSpec((None, tm, W), lambda g, i: (g, i, 0)),
        out_shape=jax.ShapeDtypeStruct((G, L, W), BF16),
        scratch_shapes=[buf] * 5,
        compiler_params=_params("arbitrary", "arbitrary"),
        name="pool",
    )(p, halo, wblk, scale)


def _merge_kernel(x_ref, ya_ref, ys_ref, yp_ref, g1_ref, wg_ref, wa_ref, wb_ref, wc_ref, wo_ref, g2_ref,
                  x1_ref, h2_ref, *, sub):
    d = x_ref.shape[-1]
    for r in range(x_ref.shape[0] // sub):
        rows = pl.ds(r * sub, sub)
        x = x_ref[rows, :]
        h = _rms(x, g1_ref[...]).astype(BF16)
        merged = (_sigmoid(_dot(h, wg_ref[:, :d])) * _dot(ya_ref[rows, :], wa_ref[...])
                  + _sigmoid(_dot(h, wg_ref[:, d:2 * d])) * _dot(ys_ref[rows, :], wb_ref[...])
                  + _sigmoid(_dot(h, wg_ref[:, 2 * d:])) * _dot(yp_ref[rows, :], wc_ref[...]))
        x1 = x + _dot(merged.astype(BF16), wo_ref[...])
        x1_ref[rows, :] = x1
        h2_ref[rows, :] = _rms(x1, g2_ref[...]).astype(BF16)


def _merge(x, ya, ys, yp, g1, wg, wa, wb, wc, wo, g2, tm):
    N, D = x.shape
    row = lambda w: pl.BlockSpec((tm, w), lambda i: (i, 0))
    consts = (g1, wg, wa, wb, wc, wo, g2)
    return pl.pallas_call(
        functools.partial(_merge_kernel, sub=min(tm, 256)),
        grid=(N // tm,),
        in_specs=[row(D), row(ATT_WIDTH), row(S5_WIDTH), row(POOL_WIDTH)] + [_resident(a.shape) for a in consts],
        out_specs=(row(D), row(D)),
        out_shape=(jax.ShapeDtypeStruct((N, D), F32), jax.ShapeDtypeStruct((N, D), BF16)),
        compiler_params=_params("arbitrary"),
        name="merge",
    )(x, ya, ys, yp, *consts)


def _ffn_kernel(h2_ref, x1_ref, halo_ref, wup_ref, cw_ref, cb_ref, wdn_ref, gfin_ref,
                x2_ref, tail_ref, ext, tail, act, *, R, tm, sub, H, fc, final_norm):
    i = pl.program_id(1)
    nt = pl.num_programs(1)
    dff = wdn_ref.shape[0]
    nc = dff // fc
    ns = tm // sub

    @pl.when(i == 0)
    def _():
        tail[...] = halo_ref[...]

    offsets = lambda c: (c * fc, dff + c * fc)

    def up(s, c):
        h2 = h2_ref[pl.ds(s * sub, sub), :]
        return [_dot(h2, wup_ref[:, off:off + fc]) for off in offsets(c)]

    def gate(s, c, us):
        halves = []
        for k, off in enumerate(offsets(c)):
            u = us[k]
            ext[k, 0:H, :] = tail[:, off:off + fc]
            ext[k, H:H + sub, :] = u
            cw = cw_ref[:, off:off + fc]
            cv = (cb_ref[:, off:off + fc] + cw[0:1, :] * ext[k, H - 2 * R:H - 2 * R + sub, :]
                  + cw[1:2, :] * ext[k, H - R:H - R + sub, :] + cw[2:3, :] * u)
            tail[:, off:off + fc] = ext[k, sub:sub + H, :]
            halves.append(cv)
        act[s, :, c * fc:(c + 1) * fc] = (_gelu(halves[0]) * halves[1]).astype(BF16)

    def down(s):
        rows = pl.ds(s * sub, sub)
        out = x1_ref[rows, :] + _dot(act[s], wdn_ref[...])
        if final_norm:
            out = _rms(out, gfin_ref[...])
        x2_ref[rows, :] = out

    cur = up(0, 0)
    for s in range(ns):
        for c in range(nc):
            if c + 1 < nc:
                nxt = up(s, c + 1)
            else:
                nxt = up(s + 1, 0) if s + 1 < ns else None
            gate(s, c, cur)
            cur = nxt
        down(s)

    @pl.when(i == nt - 1)
    def _():
        tail_ref[...] = tail[...]


def _ffn(h2, x1, halo, wup, cw, cb, wdn, gfin, *, R, tm, final_norm):
    G, L, D = x1.shape
    F2 = wup.shape[1]
    H = halo.shape[1]
    fc = 256
    sub = max(H, min(tm, 256))
    dff = wdn.shape[0]
    consts = (wup, cw, cb, wdn, gfin)
    return pl.pallas_call(
        functools.partial(_ffn_kernel, R=R, tm=tm, sub=sub, H=H, fc=fc, final_norm=final_norm),
        grid=(G, L // tm),
        in_specs=[pl.BlockSpec((None, tm, D), lambda g, i: (g, i, 0)),
                  pl.BlockSpec((None, tm, D), lambda g, i: (g, i, 0)),
                  pl.BlockSpec((None, H, F2), lambda g, i: (g, 0, 0))] + [_resident(a.shape) for a in consts],
        out_specs=(pl.BlockSpec((None, tm, D), lambda g, i: (g, i, 0)),
                   pl.BlockSpec((None, H, F2), lambda g, i: (g, 0, 0))),
        out_shape=(jax.ShapeDtypeStruct((G, L, D), F32), jax.ShapeDtypeStruct((G, H, F2), F32)),
        scratch_shapes=[pltpu.VMEM((2, H + sub, fc), F32), pltpu.VMEM((H, F2), F32),
                        pltpu.VMEM((tm // sub, sub, dff), BF16)],
        compiler_params=_params("arbitrary", "arbitrary"),
        name="ffn",
    )(h2, x1, halo, *consts)


def _blockdiag(w):
    G, a, b = w.shape
    eye = jnp.eye(G, dtype=w.dtype)
    return (w[:, :, None, :] * eye[:, None, :, None]).reshape(G * a, G * b)


def _layer_weights(l, w):
    d = w["w_in"].shape[1]
    o_f = 3 * ATT_WIDTH
    o_s = o_f + ATT_HEADS
    o_g = o_s + S5_WIDTH + POOL_WIDTH
    w_in = w["w_in"][l]
    row = lambda a: a.reshape(1, -1).astype(F32)
    bblk = jnp.concatenate([_blockdiag(jnp.swapaxes(w["s5_b_re"][l], 1, 2)),
                            _blockdiag(jnp.swapaxes(w["s5_b_im"][l], 1, 2))], axis=1)
    cblk = jnp.concatenate([_blockdiag(jnp.swapaxes(w["s5_c_re"][l], 1, 2)),
                            -_blockdiag(jnp.swapaxes(w["s5_c_im"][l], 1, 2))], axis=0)
    return dict(
        g1=row(w["norm_mix_g"][l]),
        wqkv=w_in[:, :o_f].astype(BF16),
        wf=jnp.pad(w_in[:, o_f:o_s], ((0, 0), (0, LANES - ATT_HEADS))).astype(BF16),
        wsp=w_in[:, o_s:o_g].astype(BF16),
        wg=w_in[:, o_g:].astype(BF16),
        bf=jnp.pad(row(w["b_f"][l]), ((0, 0), (0, LANES - ATT_HEADS))),
        lre=row(w["s5_lam_re"][l]), lim=row(w["s5_lam_im"][l]),
        ldt=row(jnp.broadcast_to(w["s5_log_dt"][l][:, None], (S5_GROUPS, S5_STATE))),
        bblk=bblk.astype(BF16), cblk=cblk.astype(BF16), d=row(w["s5_d"][l]),
        w1=w["s5_glu_w1"][l].astype(BF16), b1=row(w["s5_glu_b1"][l]),
        w2=w["s5_glu_w2"][l].astype(BF16), b2=row(w["s5_glu_b2"][l]),
        pw=_blockdiag(w["pool_w"][l]).astype(BF16), ps=row(w["pool_scale"][l]),
        wa=w["w_br_a"][l].astype(BF16), wb=w["w_br_b"][l].astype(BF16), wc=w["w_br_c"][l].astype(BF16),
        wo=w["w_out"][l].astype(BF16), g2=row(w["norm_ffn_g"][l]),
        wup=w["w_up"][l].astype(BF16), cw=w["conv_w"][l].astype(F32), cb=row(w["conv_b"][l]),
        wdn=w["w_down"][l].astype(BF16),
    )


def _tile(n, pref):
    return pref if n % pref == 0 else n


def kernel(x_prompt, x_sample, cache_k, cache_v, cache_logf, state_ssm_re, state_ssm_im, state_pool,
           state_ffn_conv, page_table, norm_mix_g, w_in, b_f, s5_lam_re, s5_lam_im, s5_log_dt, s5_b_re, s5_b_im,
           s5_c_re, s5_c_im, s5_d, s5_glu_w1, s5_glu_b1, s5_glu_w2, s5_glu_b2, pool_w, pool_scale, w_br_a,
           w_br_b, w_br_c, w_out, norm_ffn_g, w_up, conv_w, conv_b, w_down, norm_final_g):
    w = dict(norm_mix_g=norm_mix_g, w_in=w_in, b_f=b_f, s5_lam_re=s5_lam_re, s5_lam_im=s5_lam_im,
             s5_log_dt=s5_log_dt, s5_b_re=s5_b_re, s5_b_im=s5_b_im, s5_c_re=s5_c_re, s5_c_im=s5_c_im, s5_d=s5_d,
             s5_glu_w1=s5_glu_w1, s5_glu_b1=s5_glu_b1, s5_glu_w2=s5_glu_w2, s5_glu_b2=s5_glu_b2, pool_w=pool_w,
             pool_scale=pool_scale, w_br_a=w_br_a, w_br_b=w_br_b, w_br_c=w_br_c, w_out=w_out,
             norm_ffn_g=norm_ffn_g, w_up=w_up, conv_w=conv_w, conv_b=conv_b, w_down=w_down)
    depth = w_in.shape[0]
    B, L, D = x_prompt.shape
    Bd, T, _ = x_sample.shape
    F2 = w_up.shape[2]
    n_pool, page = cache_k.shape[1], cache_k.shape[2]
    n2 = 2 * S5_N
    gfin = norm_final_g.reshape(1, D).astype(F32)

    tm = _tile(L, 512)
    halves = 2
    Lh = L // halves
    Rp = halves * B
    assert Rp == SUBLANES, "prompt S5 scan keeps one (half, batch) group per vreg"
    Tt = _tile(Lh, 64)
    Ns = Bd * T

    ck = jnp.transpose(cache_k, (0, 1, 3, 4, 2))
    cv = jnp.transpose(cache_v, (0, 1, 3, 4, 2))
    clft = jnp.swapaxes(cache_logf, 2, 3)
    pos_r = lax.broadcasted_iota(jnp.int32, (page, page), 0)
    pos_c = lax.broadcasted_iota(jnp.int32, (page, page), 1)
    upper = (pos_r <= pos_c).astype(BF16)
    pool_halo_s = jnp.pad(jnp.swapaxes(state_pool, 1, 2), ((0, 0), (1, 0), (0, 0), (0, 0)))
    pool_halo_s = pool_halo_s.reshape(depth, 1, POOL_HALO * Bd, POOL_WIDTH)
    conv_halo_s = jnp.swapaxes(state_ffn_conv, 1, 2).reshape(depth, 1, (CONV_TAPS - 1) * Bd, F2)
    h0_s = jnp.concatenate([state_ssm_re.reshape(depth, Bd, S5_N), state_ssm_im.reshape(depth, Bd, S5_N)], axis=-1)
    h0_p = jnp.zeros((Rp, n2), F32)
    pool_halo_p = jnp.zeros((B, POOL_HALO, POOL_WIDTH), F32)
    conv_halo_p = jnp.zeros((B, SUBLANES, F2), F32)

    xp = x_prompt
    xs = jnp.swapaxes(x_sample, 0, 1).reshape(1, Ns, D)
    st_p, st_s = [], []
    for l in range(depth):
        lw = _layer_weights(l, w)
        last = l == depth - 1

        q, k, v, kb, vb, lf, c, us, up = _proj(xp, lw["g1"], lw["wqkv"], lw["wf"], lw["wsp"], lw["bf"], tm, True)
        ct = jnp.swapaxes(c, 1, 2).reshape(B, ATT_HEADS // 2, 2, L)
        ya = _attn(q, kb, vb, ct, tm)
        u_perm = us.reshape(B, halves, Lh, S5_WIDTH).transpose(2, 1, 0, 3).reshape(Lh * Rp, S5_WIDTH)
        ys_perm, hfin = _s5(u_perm, h0_p, lw["lre"], lw["lim"], lw["ldt"], lw["bblk"], lw["cblk"], lw["d"],
                            lw["w1"], lw["b1"], lw["w2"], lw["b2"], R=Rp, Tt=Tt, two_pass=True)
        ys = ys_perm.reshape(Lh, halves, B, S5_WIDTH).transpose(2, 1, 0, 3).reshape(B * L, S5_WIDTH)
        yp = _pool(up, pool_halo_p, lw["pw"], lw["ps"], R=1, tm=tm, start_pos=0)
        x1, h2 = _merge(xp.reshape(B * L, D), ya.reshape(B * L, ATT_WIDTH), ys, yp.reshape(B * L, POOL_WIDTH),
                        lw["g1"], lw["wg"], lw["wa"], lw["wb"], lw["wc"], lw["wo"], lw["g2"], tm)
        xp, tail_p = _ffn(h2.reshape(B, L, D), x1.reshape(B, L, D), conv_halo_p, lw["wup"], lw["cw"], lw["cb"],
                          lw["wdn"], gfin, R=1, tm=tm, final_norm=last)
        st_p.append((k.reshape(B, L, ATT_HEADS, HEAD_DIM), v.reshape(B, L, ATT_HEADS, HEAD_DIM), lf,
                     hfin[B:, :S5_N].reshape(B, S5_GROUPS, S5_STATE), hfin[B:, S5_N:].reshape(B, S5_GROUPS, S5_STATE),
                     up[:, L - POOL_BUF:, :], tail_p[:, SUBLANES - (CONV_TAPS - 1):, :]))

        q, k, v, _, _, lf, _, us, up = _proj(xs, lw["g1"], lw["wqkv"], lw["wf"], lw["wsp"], lw["bf"], Ns, False)
        bt = lambda a: jnp.swapaxes(a.reshape(T, Bd, -1), 0, 1)
        k_b, v_b, lf_b = bt(k), bt(v), bt(lf)
        lfn = jnp.pad(jnp.swapaxes(lf_b, 1, 2), ((0, 0), (0, 0), (0, LANES - T)))
        ya = _attn_sample(page_table, bt(q), k_b, v_b, lfn, upper, ck, cv, clft, l)
        ya = jnp.swapaxes(ya, 0, 1).reshape(Ns, ATT_WIDTH)
        ys, hfin = _s5(us.reshape(Ns, S5_WIDTH), h0_s[l], lw["lre"], lw["lim"], lw["ldt"], lw["bblk"], lw["cblk"],
                       lw["d"], lw["w1"], lw["b1"], lw["w2"], lw["b2"], R=Bd, Tt=T, two_pass=False)
        yp = _pool(up, pool_halo_s[l], lw["pw"], lw["ps"], R=Bd, tm=Ns, start_pos=page_table.shape[1] * page)
        x1, h2 = _merge(xs.reshape(Ns, D), ya, ys, yp.reshape(Ns, POOL_WIDTH),
                        lw["g1"], lw["wg"], lw["wa"], lw["wb"], lw["wc"], lw["wo"], lw["g2"], Ns)
        xs, tail_s = _ffn(h2.reshape(1, Ns, D), x1.reshape(1, Ns, D), conv_halo_s[l], lw["wup"], lw["cw"], lw["cb"],
                          lw["wdn"], gfin, R=Bd, tm=Ns, final_norm=last)
        new_pool = jnp.concatenate([state_pool[l][:, T:, :], bt(up)], axis=1)
        new_conv = jnp.swapaxes(tail_s.reshape(CONV_TAPS - 1, Bd, F2), 0, 1)
        st_s.append((k_b.reshape(Bd, T, ATT_HEADS, HEAD_DIM), v_b.reshape(Bd, T, ATT_HEADS, HEAD_DIM), lf_b,
                     hfin[:, :S5_N].reshape(Bd, S5_GROUPS, S5_STATE), hfin[:, S5_N:].reshape(Bd, S5_GROUPS, S5_STATE),
                     new_pool, new_conv))

    y_prompt = xp
    y_sample = jnp.swapaxes(xs.reshape(T, Bd, D), 0, 1)
    outs_p = [jnp.stack(z) for z in zip(*st_p)]
    outs_s = [jnp.stack(z) for z in zip(*st_s)]
    return (y_prompt, y_sample, *outs_p, *outs_s)
```

```python
import functools

import jax
import jax.numpy as jnp
from jax import lax
from jax.experimental import pallas as pl
from jax.experimental.pallas import tpu as pltpu

F32 = jnp.float32
BF16 = jnp.bfloat16

ATT_HEADS = 8
HEAD_DIM = 64
ATT_WIDTH = ATT_HEADS * HEAD_DIM
S5_WIDTH = 256
S5_GROUPS = 16
S5_STATE = 64
S5_N = S5_GROUPS * S5_STATE
POOL_WIDTH = 256
POOL_WINDOWS = (2, 4, 8, 16)
POOL_GW = 64
POOL_BUF = 15
POOL_HALO = 16
CONV_TAPS = 3
RMS_EPS = 1e-6
NEG_INF = -1e30
LANES = 128
SUBLANES = 8
VMEM_LIMIT = 56 * 1024 * 1024
HIGHEST = lax.Precision.HIGHEST
LOG2E = 1.4426950408889634


def _params(*sem):
    return pltpu.CompilerParams(dimension_semantics=sem, vmem_limit_bytes=VMEM_LIMIT)


def _resident(shape):
    nd = len(shape)
    return pl.BlockSpec(shape, lambda *_: (0,) * nd, pipeline_mode=pl.Buffered(1))


def _rms(x, g):
    return x * lax.rsqrt(jnp.mean(x * x, axis=-1, keepdims=True) + RMS_EPS) * g


def _sigmoid(x):
    return 1.0 / (1.0 + jnp.exp(-x))


def _gelu(x):
    return 0.5 * x * (1.0 + jnp.tanh(0.7978845608028654 * (x + 0.044715 * (x * x * x))))


def _log2(n):
    assert n & (n - 1) == 0, n
    return n.bit_length() - 1


def _dot(a, b):
    return jnp.dot(a, b, preferred_element_type=F32)


def _dot_nt(a, b):
    return lax.dot_general(a, b, (((1,), (1,)), ((), ())), preferred_element_type=F32)


def _proj_kernel(x_ref, g_ref, wqkv_ref, wf_ref, wsp_ref, bf_ref,
                 q_ref, k_ref, v_ref, kb_ref, vb_ref, lf_ref, c_ref, us_ref, up_ref,
                 carry_ref, *, tm, sub, with_cumsum):
    i = pl.program_id(1)

    @pl.when(i == 0)
    def _():
        carry_ref[...] = jnp.zeros_like(carry_ref)

    row = lax.broadcasted_iota(jnp.int32, (sub, sub), 0)
    col = lax.broadcasted_iota(jnp.int32, (sub, sub), 1)
    tri = (row >= col).astype(F32)
    carry = carry_ref[...]
    for r in range(tm // sub):
        rows = pl.ds(r * sub, sub)
        h = _rms(x_ref[rows, :], g_ref[...]).astype(BF16)
        qkv = _dot(h, wqkv_ref[...])
        q_ref[rows, :] = (qkv[:, :ATT_WIDTH] * (LOG2E * HEAD_DIM ** -0.5)).astype(BF16)
        k = qkv[:, ATT_WIDTH:2 * ATT_WIDTH]
        v = qkv[:, 2 * ATT_WIDTH:]
        k_ref[rows, :] = k
        v_ref[rows, :] = v
        kb_ref[rows, :] = k.astype(BF16)
        vb_ref[rows, :] = v.astype(BF16)
        f = _dot(h, wf_ref[...]) + bf_ref[...]
        lf = -(jnp.maximum(-f, 0.0) + jnp.log1p(jnp.exp(-jnp.abs(f))))
        lf_ref[rows, :] = lf[:, :ATT_HEADS]
        if with_cumsum:
            c = carry + jnp.dot(tri, lf, precision=HIGHEST, preferred_element_type=F32)
            carry = c[sub - 1:sub, :]
        else:
            c = lf
        c_ref[rows, :] = c[:, :ATT_HEADS] * LOG2E
        sp = _dot(h, wsp_ref[...])
        us_ref[rows, :] = sp[:, :S5_WIDTH]
        up_ref[rows, :] = sp[:, S5_WIDTH:]
    carry_ref[...] = carry


def _proj(x, g, wqkv, wf, wsp, bf, tm, with_cumsum):
    B, L, D = x.shape
    nt = L // tm
    row = lambda w: pl.BlockSpec((None, tm, w), lambda b, i: (b, i, 0))
    out_shape = (
        jax.ShapeDtypeStruct((B, L, ATT_WIDTH), BF16),
        jax.ShapeDtypeStruct((B, L, ATT_WIDTH), F32),
        jax.ShapeDtypeStruct((B, L, ATT_WIDTH), F32),
        jax.ShapeDtypeStruct((B, L, ATT_WIDTH), BF16),
        jax.ShapeDtypeStruct((B, L, ATT_WIDTH), BF16),
        jax.ShapeDtypeStruct((B, L, ATT_HEADS), F32),
        jax.ShapeDtypeStruct((B, L, ATT_HEADS), F32),
        jax.ShapeDtypeStruct((B, L, S5_WIDTH), F32),
        jax.ShapeDtypeStruct((B, L, POOL_WIDTH), F32),
    )
    return pl.pallas_call(
        functools.partial(_proj_kernel, tm=tm, sub=min(tm, 256), with_cumsum=with_cumsum),
        grid=(B, nt),
        in_specs=[row(D), _resident(g.shape), _resident(wqkv.shape), _resident(wf.shape),
                  _resident(wsp.shape), _resident(bf.shape)],
        out_specs=(row(ATT_WIDTH),) * 5 + (row(ATT_HEADS),) * 2 + (row(S5_WIDTH), row(POOL_WIDTH)),
        out_shape=out_shape,
        scratch_shapes=[pltpu.VMEM((1, LANES), F32)],
        compiler_params=_params("arbitrary", "arbitrary"),
        name="proj",
    )(x, g, wqkv, wf, wsp, bf)


def _attn_kernel(q_ref, k_ref, v_ref, c_ref, o_ref, qm_ref, s_ref, m_ref, acc_ref, *, tq, sub):
    qi = pl.program_id(2)
    q2 = q_ref[...]
    lo = lax.broadcasted_iota(jnp.int32, (tq, LANES), 1) < HEAD_DIM
    zero = jnp.zeros_like(q2)
    qm_ref[0] = jnp.where(lo, q2, zero)
    qm_ref[1] = jnp.where(lo, zero, q2)
    m_ref[...] = jnp.full_like(m_ref, NEG_INF)
    acc_ref[...] = jnp.zeros_like(acc_ref)

    def scores(e, j, slot):
        start = pl.multiple_of(j * tq, tq)
        s_ref[slot, e] = _dot_nt(qm_ref[e], k_ref[pl.ds(start, tq), :]) - c_ref[e:e + 1, pl.ds(start, tq)]

    def update(e, j, slot, masked):
        start = pl.multiple_of(j * tq, tq)
        vj = v_ref[pl.ds(start, tq), :]
        one = jnp.ones_like(vj)
        vje = jnp.where(lo, vj, one) if e == 0 else jnp.where(lo, one, vj)
        nsub = tq // sub if masked else 1
        rows_per = tq // nsub
        for r in range(nsub):
            rows = pl.ds(r * rows_per, rows_per)
            width = (r + 1) * sub if masked else tq
            s = s_ref[slot, e, rows, 0:width]
            if masked:
                row = lax.broadcasted_iota(jnp.int32, (rows_per, width), 0) + r * sub
                col = lax.broadcasted_iota(jnp.int32, (rows_per, width), 1)
                s = jnp.where(col <= row, s, NEG_INF)
            m_old = m_ref[e, rows, :]
            m_new = jnp.maximum(m_old, jnp.max(s, axis=1, keepdims=True))
            alpha = jnp.exp2(m_old - m_new)
            p = jnp.exp2(s - jnp.concatenate([m_new] * (width // LANES), axis=1))
            acc_ref[e, rows, :] = alpha * acc_ref[e, rows, :] + _dot(p.astype(BF16), vje[0:width, :])
            m_ref[e, rows, :] = m_new

    scores(0, 0, 0)
    scores(1, 0, 0)

    def step(j, slot):
        for e in range(2):
            scores(e, j + 1, 1 - slot)
            update(e, j, slot, False)

    def body(jj, carry):
        step(2 * jj, 0)
        step(2 * jj + 1, 1)
        return carry

    npair = jnp.right_shift(qi, 1)
    lax.fori_loop(0, npair, body, 0)

    @pl.when(qi == 2 * npair)
    def _():
        update(0, qi, 0, True)
        update(1, qi, 0, True)

    @pl.when(qi != 2 * npair)
    def _():
        step(qi - 1, 0)
        update(0, qi, 1, True)
        update(1, qi, 1, True)

    a0 = acc_ref[0]
    a1 = acc_ref[1]
    l0 = pltpu.roll(a0, HEAD_DIM, axis=1)
    l1 = pltpu.roll(a1, HEAD_DIM, axis=1)
    o_ref[...] = jnp.where(lo, a0 / l0, a1 / l1).astype(BF16)


def _attn(q, kb, vb, ct, tq):
    B, L, _ = q.shape
    nq = L // tq
    npair = ATT_HEADS // 2
    return pl.pallas_call(
        functools.partial(_attn_kernel, tq=tq, sub=min(tq, LANES)),
        grid=(B, npair, nq),
        in_specs=[
            pl.BlockSpec((None, tq, LANES), lambda b, hp, qi: (b, qi, hp)),
            pl.BlockSpec((None, L, LANES), lambda b, hp, qi: (b, 0, hp)),
            pl.BlockSpec((None, L, LANES), lambda b, hp, qi: (b, 0, hp)),
            pl.BlockSpec((None, None, 2, L), lambda b, hp, qi: (b, hp, 0, 0)),
        ],
        out_specs=pl.BlockSpec((None, tq, LANES), lambda b, hp, qi: (b, qi, hp)),
        out_shape=jax.ShapeDtypeStruct((B, L, ATT_WIDTH), BF16),
        scratch_shapes=[pltpu.VMEM((2, tq, LANES), BF16), pltpu.VMEM((2, 2, tq, tq), F32),
                        pltpu.VMEM((2, tq, LANES), F32), pltpu.VMEM((2, tq, LANES), F32)],
        compiler_params=_params("arbitrary", "arbitrary", "arbitrary"),
        name="attn_prompt",
    )(q, kb, vb, ct)


def _split3_dot(a, b):
    a1 = a.astype(BF16)
    r1 = a - a1.astype(F32)
    a2 = r1.astype(BF16)
    a3 = (r1 - a2.astype(F32)).astype(BF16)
    return _dot(a1, b) + _dot(a2, b) + _dot(a3, b)


def _attn_sample_kernel(pt_ref, q_ref, kn_ref, vn_ref, lfn_ref, u_ref, *rest, n_pages, page, t_new):
    del pt_ref
    k_refs = rest[:n_pages]
    v_refs = rest[n_pages:2 * n_pages]
    lf_refs = rest[2 * n_pages:3 * n_pages]
    o_ref = rest[3 * n_pages]
    nh = ATT_HEADS
    nrow = t_new * nh
    sub = lax.broadcasted_iota(jnp.int32, (nh, ATT_WIDTH), 0)
    lane = lax.broadcasted_iota(jnp.int32, (nh, ATT_WIDTH), 1)
    head_mask = jnp.right_shift(lane, _log2(HEAD_DIM)) == sub
    q = q_ref[...].astype(F32)
    qexp = jnp.concatenate(
        [jnp.where(head_mask, jnp.broadcast_to(q[t:t + 1, :], (nh, ATT_WIDTH)), 0.0) for t in range(t_new)],
        axis=0)
    qexp_b = qexp.astype(BF16)

    lf = jnp.concatenate([r[...] for r in lf_refs], axis=0)
    local = _split3_dot(lf, u_ref[...])
    carry = jnp.zeros((nh, 1), F32)
    cum = []
    for j in range(n_pages):
        cj = local[j * nh:(j + 1) * nh, :] + carry
        carry = cj[:, page - 1:page]
        cum.append(cj)
    cum = jnp.concatenate(cum, axis=0) * LOG2E

    s_pages = []
    for j in range(n_pages):
        kt = k_refs[j][...].reshape(ATT_WIDTH, page).astype(BF16)
        bj = jnp.concatenate([cum[j * nh:(j + 1) * nh, :]] * t_new, axis=0)
        s_pages.append(_dot(qexp_b, kt) - bj)

    kn = kn_ref[...].astype(BF16).astype(F32)
    vn = vn_ref[...].astype(BF16).astype(F32)
    lfn = lfn_ref[...]
    tq_of_row = jnp.right_shift(lax.broadcasted_iota(jnp.int32, (nrow, 1), 0), _log2(nh))
    s_new = []
    for t in range(t_new):
        carry = carry + lfn[:, t:t + 1]
        st = (jnp.sum(qexp * kn[t:t + 1, :], axis=-1, keepdims=True)
              - jnp.concatenate([carry * LOG2E] * t_new, axis=0))
        s_new.append(jnp.where(tq_of_row >= t, st, NEG_INF))

    m = s_new[0]
    for st in s_new[1:]:
        m = jnp.maximum(m, st)
    for s in s_pages:
        m = jnp.maximum(m, jnp.max(s, axis=-1, keepdims=True))
    l = jnp.zeros((nrow, 1), F32)
    acc = jnp.zeros((nrow, ATT_WIDTH), F32)
    for j in range(n_pages):
        p = jnp.exp2(s_pages[j] - m)
        l = l + jnp.sum(p, axis=-1, keepdims=True)
        vt = v_refs[j][...].reshape(ATT_WIDTH, page).astype(BF16)
        acc = acc + _dot_nt(p.astype(BF16), vt)
    for t in range(t_new):
        p = jnp.exp2(s_new[t] - m)
        l = l + p
        acc = acc + p.astype(BF16).astype(F32) * vn[t:t + 1, :]
    out = acc / l
    rows = [jnp.sum(jnp.where(head_mask, out[t * nh:(t + 1) * nh, :], 0.0), axis=0, keepdims=True)
            for t in range(t_new)]
    o_ref[...] = jnp.concatenate(rows, axis=0).astype(BF16)


def _attn_sample(page_table, q, kn, vn, lfn, upper, cache_kt, cache_vt, cache_lft, layer):
    Bd, T, _ = q.shape
    n_pages = page_table.shape[1]
    page = cache_lft.shape[3]

    def page_spec(shape, j):
        return pl.BlockSpec((None, None) + shape, lambda b, pt: (layer, pt[b, j]) + (0,) * len(shape))

    seq = lambda shape: pl.BlockSpec((None,) + shape, lambda b, pt: (b, 0, 0))
    const = lambda a: pl.BlockSpec(a.shape, lambda b, pt: (0, 0), pipeline_mode=pl.Buffered(1))
    in_specs = [seq((T, ATT_WIDTH))] * 3 + [seq((ATT_HEADS, LANES)), const(upper)]
    in_specs += [page_spec((ATT_HEADS, HEAD_DIM, page), j) for j in range(n_pages)]
    in_specs += [page_spec((ATT_HEADS, HEAD_DIM, page), j) for j in range(n_pages)]
    in_specs += [page_spec((ATT_HEADS, page), j) for j in range(n_pages)]
    grid_spec = pltpu.PrefetchScalarGridSpec(
        num_scalar_prefetch=1, grid=(Bd,), in_specs=in_specs,
        out_specs=pl.BlockSpec((None, T, ATT_WIDTH), lambda b, pt: (b, 0, 0)))
    return pl.pallas_call(
        functools.partial(_attn_sample_kernel, n_pages=n_pages, page=page, t_new=T),
        grid_spec=grid_spec,
        out_shape=jax.ShapeDtypeStruct((Bd, T, ATT_WIDTH), BF16),
        compiler_params=_params("arbitrary"),
        name="attn_sample",
    )(page_table, q, kn, vn, lfn, upper,
      *([cache_kt] * n_pages), *([cache_vt] * n_pages), *([cache_lft] * n_pages))


def _split3(x):
    x1 = x.astype(BF16)
    r1 = x - x1.astype(F32)
    x2 = r1.astype(BF16)
    return x1, x2, (r1 - x2.astype(F32)).astype(BF16)


def _s5_kernel(*refs, R, Tt, two_pass, perm):
    if perm:
        u_ref, pm_ref, pmt_ref, *refs = refs
    else:
        u_ref, *refs = refs
    (h0_ref, lre_ref, lim_ref, ldt_ref, bblk_ref, cblk_ref, d_ref, w1_ref, b1_ref, w2_ref, b2_ref,
     y_ref, hfin_ref, x_scr, h_scr, st_scr, end_scr, bz_scr) = refs
    p = pl.program_id(0)
    i = pl.program_id(1)
    nt = pl.num_programs(1)
    n = S5_N
    last_pass = 1 if two_pass else 0

    dt = jnp.exp(ldt_ref[...])
    lr = lre_ref[...]
    li = lim_ref[...]
    mag = jnp.exp(lr * dt)
    a_re = mag * jnp.cos(li * dt)
    a_im = mag * jnp.sin(li * dt)
    den = lr * lr + li * li
    z_re = ((a_re - 1.0) * lr + a_im * li) / den
    z_im = (a_im * lr - (a_re - 1.0) * li) / den

    @pl.when((i == 0) & (p == 0))
    def _():
        st_scr[...] = h0_ref[...]
        b_re = bblk_ref[:, :n]
        b_im = bblk_ref[:, n:]
        bz_scr[:, :n] = (z_re * b_re - z_im * b_im).astype(BF16)
        bz_scr[:, n:] = (z_re * b_im + z_im * b_re).astype(BF16)

    if perm:
        u_parts = _split3(u_ref[...].reshape(R * Tt, S5_WIDTH))
        u_b = _dot(pm_ref[...], u_parts[0]).astype(BF16)
    else:
        u_b = u_ref[...].astype(BF16)
    x_scr[...] = _dot(u_b, bz_scr[...])

    if two_pass:
        @pl.when((i == 0) & (p == 1))
        def _():
            rolled = pltpu.roll(end_scr[...], R // 2, axis=0)
            first = lax.broadcasted_iota(jnp.int32, (R, 2 * n), 0) < R // 2
            st_scr[...] = jnp.where(first, h0_ref[...], rolled)

    ar = jnp.broadcast_to(a_re, (R, n))
    ai = jnp.broadcast_to(a_im, (R, n))

    def scan(store):
        def step(t, carry):
            hr, hi = carry
            r0 = t * R if isinstance(t, int) else pl.multiple_of(t * R, R)
            xr = x_scr[pl.ds(r0, R), :n]
            xi = x_scr[pl.ds(r0, R), n:]
            nr = ar * hr - ai * hi + xr
            ni = ar * hi + ai * hr + xi
            if store:
                h_scr[pl.ds(r0, R), :n] = nr
                h_scr[pl.ds(r0, R), n:] = ni
            return nr, ni

        if R == SUBLANES:
            hr, hi = lax.fori_loop(0, Tt, step, (st_scr[:, :n], st_scr[:, n:]), unroll=2)
        else:
            hr, hi = st_scr[:, :n], st_scr[:, n:]
            for t in range(Tt):
                hr, hi = step(t, (hr, hi))
        st_scr[:, :n] = hr
        st_scr[:, n:] = hi

    if two_pass:
        @pl.when(p == 0)
        def _():
            scan(False)

            @pl.when(i == nt - 1)
            def _():
                end_scr[...] = st_scr[...]

    @pl.when(p == last_pass)
    def _():
        scan(True)
        if perm:
            u = u_b.astype(F32) + _dot(pm_ref[...], u_parts[1]) + _dot(pm_ref[...], u_parts[2])
        else:
            u = u_ref[...]
        y = _dot(h_scr[...].astype(BF16), cblk_ref[...]) + d_ref[...] * u
        z = _gelu(y).astype(BF16)
        out = ((_dot(z, w1_ref[...]) + b1_ref[...]) * _sigmoid(_dot(z, w2_ref[...]) + b2_ref[...])).astype(BF16)
        if perm:
            y_ref[...] = _dot(pmt_ref[...], out).astype(BF16).reshape(y_ref.shape)
        else:
            y_ref[...] = out

        @pl.when(i == nt - 1)
        def _():
            hfin_ref[...] = st_scr[...]


def _s5(u, h0, lre, lim, ldt, bblk, cblk, d, w1, b1, w2, b2, *, R, Tt, two_pass, perm=None):
    tile = R * Tt
    npass = 2 if two_pass else 1
    n2 = 2 * S5_N
    step = lambda p, i: i * p if two_pass else i
    if perm is None:
        nt = u.shape[0] // tile
        u_spec = pl.BlockSpec((tile, S5_WIDTH), lambda p, i: (i, 0))
        y_spec = pl.BlockSpec((tile, S5_WIDTH), lambda p, i: (step(p, i), 0))
        lead = ()
    else:
        nb, nh = u.shape[:2]
        nt = u.shape[2] // Tt
        u_spec = pl.BlockSpec((nb, nh, Tt, S5_WIDTH), lambda p, i: (0, 0, i, 0))
        y_spec = pl.BlockSpec((nb, nh, Tt, S5_WIDTH), lambda p, i: (0, 0, step(p, i), 0))
        lead = tuple(perm)
    consts = lead + (h0, lre, lim, ldt, bblk, cblk, d, w1, b1, w2, b2)
    return pl.pallas_call(
        functools.partial(_s5_kernel, R=R, Tt=Tt, two_pass=two_pass, perm=perm is not None),
        grid=(npass, nt),
        in_specs=[u_spec] + [_resident(a.shape) for a in consts],
        out_specs=(y_spec, pl.BlockSpec((R, n2), lambda p, i: (0, 0))),
        out_shape=(jax.ShapeDtypeStruct(u.shape, BF16), jax.ShapeDtypeStruct((R, n2), F32)),
        scratch_shapes=[pltpu.VMEM((tile, n2), F32), pltpu.VMEM((tile, n2), F32),
                        pltpu.VMEM((R, n2), F32), pltpu.VMEM((R, n2), F32),
                        pltpu.VMEM((S5_WIDTH, n2), BF16)],
        compiler_params=_params("arbitrary", "arbitrary"),
        name="s5",
    )(u, *consts)


def _pool_kernel(p_ref, halo_ref, w_ref, sc_ref, y_ref, ext, a2, a4, a8, a16, *, R, tm, start_pos, nt):
    i = pl.program_id(1)
    H = POOL_HALO * R
    n = H + tm

    @pl.when(i == 0)
    def _():
        ext[0:H, :] = halo_ref[...]

    x = p_ref[...]
    ext[H:n, :] = x
    n2, n4, n8, n16 = n - R, n - 3 * R, n - 7 * R, n - 15 * R
    a2[0:n2, :] = ext[R:R + n2, :] + ext[0:n2, :]
    a4[0:n4, :] = a2[2 * R:2 * R + n4, :] + a2[0:n4, :]
    a8[0:n8, :] = a4[4 * R:4 * R + n8, :] + a4[0:n8, :]
    a16[0:n16, :] = a8[8 * R:8 * R + n16, :] + a8[0:n16, :]
    s2 = a2[15 * R:15 * R + tm, :]
    s4 = a4[13 * R:13 * R + tm, :]
    s8 = a8[9 * R:9 * R + tm, :]
    s16 = a16[R:R + tm, :]
    pos = start_pos + jnp.right_shift(i * tm + lax.broadcasted_iota(jnp.int32, (tm, POOL_WIDTH), 0), _log2(R))
    lane = lax.broadcasted_iota(jnp.int32, (tm, POOL_WIDTH), 1)
    cnt = lambda w: jnp.minimum(w, pos + 1).astype(F32)
    mean = jnp.where(lane < POOL_GW, s2 / cnt(2),
                     jnp.where(lane < 2 * POOL_GW, s4 / cnt(4),
                               jnp.where(lane < 3 * POOL_GW, s8 / cnt(8), s16 / cnt(16))))
    m = (mean - x).astype(BF16)
    y_ref[...] = (_dot(m, w_ref[...]) * sc_ref[...]).astype(BF16)
    if nt > 1:
        ext[0:H, :] = ext[tm:tm + H, :]


def _pool(p, halo, wblk, scale, *, R, tm, start_pos):
    G, L, W = p.shape
    nt = L // tm
    H = POOL_HALO * R
    buf = pltpu.VMEM((H + tm, W), F32)
    return pl.pallas_call(
        functools.partial(_pool_kernel, R=R, tm=tm, start_pos=start_pos, nt=nt),
        grid=(G, nt),
        in_specs=[pl.BlockSpec((None, tm, W), lambda g, i: (g, i, 0)),
                  pl.BlockSpec((None, H, W), lambda g, i: (g, 0, 0)),
                  _resident(wblk.shape), _resident(scale.shape)],
        out_specs=pl.BlockSpec((None, tm, W), lambda g, i: (g, i, 0)),
        out_shape=jax.ShapeDtypeStruct((G, L, W), BF16),
        scratch_shapes=[buf] * 5,
        compiler_params=_params("arbitrary", "arbitrary"),
        name="pool",
    )(p, halo, wblk, scale)


def _merge_kernel(x_ref, ya_ref, ys_ref, yp_ref, g1_ref, wg_ref, wa_ref, wb_ref, wc_ref, wo_ref, g2_ref,
                  x1_ref, h2_ref, *, sub):
    d = x_ref.shape[-1]
    for r in range(x_ref.shape[0] // sub):
        rows = pl.ds(r * sub, sub)
        x = x_ref[rows, :]
        h = _rms(x, g1_ref[...]).astype(BF16)
        merged = (_sigmoid(_dot(h, wg_ref[:, :d])) * _dot(ya_ref[rows, :], wa_ref[...])
                  + _sigmoid(_dot(h, wg_ref[:, d:2 * d])) * _dot(ys_ref[rows, :], wb_ref[...])
                  + _sigmoid(_dot(h, wg_ref[:, 2 * d:])) * _dot(yp_ref[rows, :], wc_ref[...]))
        x1 = x + _dot(merged.astype(BF16), wo_ref[...])
        x1_ref[rows, :] = x1
        h2_ref[rows, :] = _rms(x1, g2_ref[...]).astype(BF16)


def _merge(x, ya, ys, yp, g1, wg, wa, wb, wc, wo, g2, tm):
    N, D = x.shape
    row = lambda w: pl.BlockSpec((tm, w), lambda i: (i, 0))
    consts = (g1, wg, wa, wb, wc, wo, g2)
    return pl.pallas_call(
        functools.partial(_merge_kernel, sub=min(tm, 256)),
        grid=(N // tm,),
        in_specs=[row(D), row(ATT_WIDTH), row(S5_WIDTH), row(POOL_WIDTH)] + [_resident(a.shape) for a in consts],
        out_specs=(row(D), row(D)),
        out_shape=(jax.ShapeDtypeStruct((N, D), F32), jax.ShapeDtypeStruct((N, D), BF16)),
        compiler_params=_params("arbitrary"),
        name="merge",
    )(x, ya, ys, yp, *consts)


def _ffn_kernel(h2_ref, x1_ref, halo_ref, wup_ref, cw_ref, cb_ref, wdn_ref, gfin_ref,
                x2_ref, tail_ref, ext, tail, act, *, R, tm, sub, H, fc, final_norm):
    i = pl.program_id(1)
    nt = pl.num_programs(1)
    dff = wdn_ref.shape[0]
    nc = dff // fc
    ns = tm // sub

    @pl.when(i == 0)
    def _():
        tail[...] = halo_ref[...]

    offsets = lambda c: (c * fc, dff + c * fc)

    def up(s, c):
        h2 = h2_ref[pl.ds(s * sub, sub), :]
        return [_dot(h2, wup_ref[:, off:off + fc]) for off in offsets(c)]

    def gate(s, c, us):
        halves = []
        for k, off in enumerate(offsets(c)):
            u = us[k]
            ext[k, 0:H, :] = tail[:, off:off + fc]
            ext[k, H:H + sub, :] = u
            cw = cw_ref[:, off:off + fc]
            cv = (cb_ref[:, off:off + fc] + cw[0:1, :] * ext[k, H - 2 * R:H - 2 * R + sub, :]
                  + cw[1:2, :] * ext[k, H - R:H - R + sub, :] + cw[2:3, :] * u)
            tail[:, off:off + fc] = ext[k, sub:sub + H, :]
            halves.append(cv)
        act[s, :, c * fc:(c + 1) * fc] = (_gelu(halves[0]) * halves[1]).astype(BF16)

    def down(s):
        rows = pl.ds(s * sub, sub)
        out = x1_ref[rows, :] + _dot(act[s], wdn_ref[...])
        if final_norm:
            out = _rms(out, gfin_ref[...])
        x2_ref[rows, :] = out

    cur = up(0, 0)
    for s in range(ns):
        for c in range(nc):
            if c + 1 < nc:
                nxt = up(s, c + 1)
            else:
                nxt = up(s + 1, 0) if s + 1 < ns else None
            gate(s, c, cur)
            cur = nxt
        down(s)

    @pl.when(i == nt - 1)
    def _():
        tail_ref[...] = tail[...]


def _ffn(h2, x1, halo, wup, cw, cb, wdn, gfin, *, R, tm, final_norm):
    G, L, D = x1.shape
    F2 = wup.shape[1]
    H = halo.shape[1]
    fc = 256
    sub = max(H, min(tm, 256))
    dff = wdn.shape[0]
    consts = (wup, cw, cb, wdn, gfin)
    return pl.pallas_call(
        functools.partial(_ffn_kernel, R=R, tm=tm, sub=sub, H=H, fc=fc, final_norm=final_norm),
        grid=(G, L // tm),
        in_specs=[pl.BlockSpec((None, tm, D), lambda g, i: (g, i, 0)),
                  pl.BlockSpec((None, tm, D), lambda g, i: (g, i, 0)),
                  pl.BlockSpec((None, H, F2), lambda g, i: (g, 0, 0))] + [_resident(a.shape) for a in consts],
        out_specs=(pl.BlockSpec((None, tm, D), lambda g, i: (g, i, 0)),
                   pl.BlockSpec((None, H, F2), lambda g, i: (g, 0, 0))),
        out_shape=(jax.ShapeDtypeStruct((G, L, D), F32), jax.ShapeDtypeStruct((G, H, F2), F32)),
        scratch_shapes=[pltpu.VMEM((2, H + sub, fc), F32), pltpu.VMEM((H, F2), F32),
                        pltpu.VMEM((tm // sub, sub, dff), BF16)],
        compiler_params=_params("arbitrary", "arbitrary"),
        name="ffn",
    )(h2, x1, halo, *consts)


def _blockdiag(w):
    G, a, b = w.shape
    eye = jnp.eye(G, dtype=w.dtype)
    return (w[:, :, None, :] * eye[:, None, :, None]).reshape(G * a, G * b)


def _layer_weights(l, w):
    d = w["w_in"].shape[1]
    o_f = 3 * ATT_WIDTH
    o_s = o_f + ATT_HEADS
    o_g = o_s + S5_WIDTH + POOL_WIDTH
    w_in = w["w_in"][l]
    row = lambda a: a.reshape(1, -1).astype(F32)
    bblk = jnp.concatenate([_blockdiag(jnp.swapaxes(w["s5_b_re"][l], 1, 2)),
                            _blockdiag(jnp.swapaxes(w["s5_b_im"][l], 1, 2))], axis=1)
    cblk = jnp.concatenate([_blockdiag(jnp.swapaxes(w["s5_c_re"][l], 1, 2)),
                            -_blockdiag(jnp.swapaxes(w["s5_c_im"][l], 1, 2))], axis=0)
    return dict(
        g1=row(w["norm_mix_g"][l]),
        wqkv=w_in[:, :o_f].astype(BF16),
        wf=jnp.pad(w_in[:, o_f:o_s], ((0, 0), (0, LANES - ATT_HEADS))).astype(BF16),
        wsp=w_in[:, o_s:o_g].astype(BF16),
        wg=w_in[:, o_g:].astype(BF16),
        bf=jnp.pad(row(w["b_f"][l]), ((0, 0), (0, LANES - ATT_HEADS))),
        lre=row(w["s5_lam_re"][l]), lim=row(w["s5_lam_im"][l]),
        ldt=row(jnp.broadcast_to(w["s5_log_dt"][l][:, None], (S5_GROUPS, S5_STATE))),
        bblk=bblk.astype(F32), cblk=cblk.astype(BF16), d=row(w["s5_d"][l]),
        w1=w["s5_glu_w1"][l].astype(BF16), b1=row(w["s5_glu_b1"][l]),
        w2=w["s5_glu_w2"][l].astype(BF16), b2=row(w["s5_glu_b2"][l]),
        pw=_blockdiag(w["pool_w"][l]).astype(BF16), ps=row(w["pool_scale"][l]),
        wa=w["w_br_a"][l].astype(BF16), wb=w["w_br_b"][l].astype(BF16), wc=w["w_br_c"][l].astype(BF16),
        wo=w["w_out"][l].astype(BF16), g2=row(w["norm_ffn_g"][l]),
        wup=w["w_up"][l].astype(BF16), cw=w["conv_w"][l].astype(F32), cb=row(w["conv_b"][l]),
        wdn=w["w_down"][l].astype(BF16),
    )


def _tile(n, pref):
    return pref if n % pref == 0 else n


def kernel(x_prompt, x_sample, cache_k, cache_v, cache_logf, state_ssm_re, state_ssm_im, state_pool,
           state_ffn_conv, page_table, norm_mix_g, w_in, b_f, s5_lam_re, s5_lam_im, s5_log_dt, s5_b_re, s5_b_im,
           s5_c_re, s5_c_im, s5_d, s5_glu_w1, s5_glu_b1, s5_glu_w2, s5_glu_b2, pool_w, pool_scale, w_br_a,
           w_br_b, w_br_c, w_out, norm_ffn_g, w_up, conv_w, conv_b, w_down, norm_final_g):
    w = dict(norm_mix_g=norm_mix_g, w_in=w_in, b_f=b_f, s5_lam_re=s5_lam_re, s5_lam_im=s5_lam_im,
             s5_log_dt=s5_log_dt, s5_b_re=s5_b_re, s5_b_im=s5_b_im, s5_c_re=s5_c_re, s5_c_im=s5_c_im, s5_d=s5_d,
             s5_glu_w1=s5_glu_w1, s5_glu_b1=s5_glu_b1, s5_glu_w2=s5_glu_w2, s5_glu_b2=s5_glu_b2, pool_w=pool_w,
             pool_scale=pool_scale, w_br_a=w_br_a, w_br_b=w_br_b, w_br_c=w_br_c, w_out=w_out,
             norm_ffn_g=norm_ffn_g, w_up=w_up, conv_w=conv_w, conv_b=conv_b, w_down=w_down)
    depth = w_in.shape[0]
    B, L, D = x_prompt.shape
    Bd, T, _ = x_sample.shape
    F2 = w_up.shape[2]
    n_pool, page = cache_k.shape[1], cache_k.shape[2]
    n2 = 2 * S5_N
    gfin = norm_final_g.reshape(1, D).astype(F32)

    tm = _tile(L, 512)
    halves = 2
    Lh = L // halves
    Rp = halves * B
    assert Rp == SUBLANES, "prompt S5 scan keeps one (half, batch) group per vreg"
    Tt = _tile(Lh, 64)
    Ns = Bd * T

    ck = jnp.transpose(cache_k, (0, 1, 3, 4, 2))
    cv = jnp.transpose(cache_v, (0, 1, 3, 4, 2))
    clft = jnp.swapaxes(cache_logf, 2, 3)
    pos_r = lax.broadcasted_iota(jnp.int32, (page, page), 0)
    pos_c = lax.broadcasted_iota(jnp.int32, (page, page), 1)
    upper = (pos_r <= pos_c).astype(BF16)
    pool_halo_s = jnp.pad(jnp.swapaxes(state_pool, 1, 2), ((0, 0), (1, 0), (0, 0), (0, 0)))
    pool_halo_s = pool_halo_s.reshape(depth, 1, POOL_HALO * Bd, POOL_WIDTH)
    conv_halo_s = jnp.swapaxes(state_ffn_conv, 1, 2).reshape(depth, 1, (CONV_TAPS - 1) * Bd, F2)
    h0_s = jnp.concatenate([state_ssm_re.reshape(depth, Bd, S5_N), state_ssm_im.reshape(depth, Bd, S5_N)], axis=-1)
    h0_p = jnp.zeros((Rp, n2), F32)
    dst = lax.broadcasted_iota(jnp.int32, (Rp * Tt, Rp * Tt), 0)
    src = lax.broadcasted_iota(jnp.int32, (Rp * Tt, Rp * Tt), 1)
    seg, t_in = dst % Rp, dst // Rp
    pm = (src == ((seg % B) * halves + seg // B) * Tt + t_in).astype(BF16)
    s5_perm = (pm, pm.T)
    pool_halo_p = jnp.zeros((B, POOL_HALO, POOL_WIDTH), F32)
    conv_halo_p = jnp.zeros((B, SUBLANES, F2), F32)

    xp = x_prompt
    xs = jnp.swapaxes(x_sample, 0, 1).reshape(1, Ns, D)
    st_p, st_s = [], []
    for l in range(depth):
        lw = _layer_weights(l, w)
        last = l == depth - 1

        q, k, v, kb, vb, lf, c, us, up = _proj(xp, lw["g1"], lw["wqkv"], lw["wf"], lw["wsp"], lw["bf"], tm, True)
        ct = jnp.swapaxes(c, 1, 2).reshape(B, ATT_HEADS // 2, 2, L)
        ya = _attn(q, kb, vb, ct, tm)
        ys, hfin = _s5(us.reshape(B, halves, Lh, S5_WIDTH), h0_p, lw["lre"], lw["lim"], lw["ldt"], lw["bblk"],
                       lw["cblk"], lw["d"], lw["w1"], lw["b1"], lw["w2"], lw["b2"], R=Rp, Tt=Tt, two_pass=True,
                       perm=s5_perm)
        ys = ys.reshape(B * L, S5_WIDTH)
        yp = _pool(up, pool_halo_p, lw["pw"], lw["ps"], R=1, tm=tm, start_pos=0)
        x1, h2 = _merge(xp.reshape(B * L, D), ya.reshape(B * L, ATT_WIDTH), ys, yp.reshape(B * L, POOL_WIDTH),
                        lw["g1"], lw["wg"], lw["wa"], lw["wb"], lw["wc"], lw["wo"], lw["g2"], tm)
        xp, tail_p = _ffn(h2.reshape(B, L, D), x1.reshape(B, L, D), conv_halo_p, lw["wup"], lw["cw"], lw["cb"],
                          lw["wdn"], gfin, R=1, tm=tm, final_norm=last)
        st_p.append((k.reshape(B, L, ATT_HEADS, HEAD_DIM), v.reshape(B, L, ATT_HEADS, HEAD_DIM), lf,
                     hfin[B:, :S5_N].reshape(B, S5_GROUPS, S5_STATE), hfin[B:, S5_N:].reshape(B, S5_GROUPS, S5_STATE),
                     up[:, L - POOL_BUF:, :], tail_p[:, SUBLANES - (CONV_TAPS - 1):, :]))

        q, k, v, _, _, lf, _, us, up = _proj(xs, lw["g1"], lw["wqkv"], lw["wf"], lw["wsp"], lw["bf"], Ns, False)
        bt = lambda a: jnp.swapaxes(a.reshape(T, Bd, -1), 0, 1)
        k_b, v_b, lf_b = bt(k), bt(v), bt(lf)
        lfn = jnp.pad(jnp.swapaxes(lf_b, 1, 2), ((0, 0), (0, 0), (0, LANES - T)))
        ya = _attn_sample(page_table, bt(q), k_b, v_b, lfn, upper, ck, cv, clft, l)
        ya = jnp.swapaxes(ya, 0, 1).reshape(Ns, ATT_WIDTH)
        ys, hfin = _s5(us.reshape(Ns, S5_WIDTH), h0_s[l], lw["lre"], lw["lim"], lw["ldt"], lw["bblk"], lw["cblk"],
                       lw["d"], lw["w1"], lw["b1"], lw["w2"], lw["b2"], R=Bd, Tt=T, two_pass=False)
        yp = _pool(up, pool_halo_s[l], lw["pw"], lw["ps"], R=Bd, tm=Ns, start_pos=page_table.shape[1] * page)
        x1, h2 = _merge(xs.reshape(Ns, D), ya, ys, yp.reshape(Ns, POOL_WIDTH),
                        lw["g1"], lw["wg"], lw["wa"], lw["wb"], lw["wc"], lw["wo"], lw["g2"], Ns)
        xs, tail_s = _ffn(h2.reshape(1, Ns, D), x1.reshape(1, Ns, D), conv_halo_s[l], lw["wup"], lw["cw"], lw["cb"],
                          lw["wdn"], gfin, R=Bd, tm=Ns, final_norm=last)
        new_pool = jnp.concatenate([state_pool[l][:, T:, :], bt(up)], axis=1)
        new_conv = jnp.swapaxes(tail_s.reshape(CONV_TAPS - 1, Bd, F2), 0, 1)
        st_s.append((k_b.reshape(Bd, T, ATT_HEADS, HEAD_DIM), v_b.reshape(Bd, T, ATT_HEADS, HEAD_DIM), lf_b,
                     hfin[:, :S5_N].reshape(Bd, S5_GROUPS, S5_STATE), hfin[:, S5_N:].reshape(Bd, S5_GROUPS, S5_STATE),
                     new_pool, new_conv))

    y_prompt = xp
    y_sample = jnp.swapaxes(xs.reshape(T, Bd, D), 0, 1)
    outs_p = [jnp.stack(z) for z in zip(*st_p)]
    outs_s = [jnp.stack(z) for z in zip(*st_s)]
    return (y_prompt, y_sample, *outs_p, *outs_s)
```

```python
import functools

import jax
import jax.numpy as jnp
from jax import lax
from jax.experimental import pallas as pl
from jax.experimental.pallas import tpu as pltpu

F32 = jnp.float32
BF16 = jnp.bfloat16

ATT_HEADS = 8
HEAD_DIM = 64
ATT_WIDTH = ATT_HEADS * HEAD_DIM
S5_WIDTH = 256
S5_GROUPS = 16
S5_STATE = 64
S5_N = S5_GROUPS * S5_STATE
POOL_WIDTH = 256
POOL_WINDOWS = (2, 4, 8, 16)
POOL_GW = 64
POOL_BUF = 15
POOL_HALO = 16
CONV_TAPS = 3
RMS_EPS = 1e-6
NEG_INF = -1e30
LANES = 128
SUBLANES = 8
VMEM_LIMIT = 56 * 1024 * 1024
HIGHEST = lax.Precision.HIGHEST
LOG2E = 1.4426950408889634


def _params(*sem):
    return pltpu.CompilerParams(dimension_semantics=sem, vmem_limit_bytes=VMEM_LIMIT)


def _resident(shape):
    nd = len(shape)
    return pl.BlockSpec(shape, lambda *_: (0,) * nd, pipeline_mode=pl.Buffered(1))


def _rms(x, g):
    return x * lax.rsqrt(jnp.mean(x * x, axis=-1, keepdims=True) + RMS_EPS) * g


def _sigmoid(x):
    return 1.0 / (1.0 + jnp.exp(-x))


def _gelu(x):
    return 0.5 * x * (1.0 + jnp.tanh(0.7978845608028654 * (x + 0.044715 * (x * x * x))))


def _log2(n):
    assert n & (n - 1) == 0, n
    return n.bit_length() - 1


def _dot(a, b):
    return jnp.dot(a, b, preferred_element_type=F32)


def _dot_nt(a, b):
    return lax.dot_general(a, b, (((1,), (1,)), ((), ())), preferred_element_type=F32)


def _proj_kernel(x_ref, g_ref, wqkv_ref, wf_ref, wsp_ref, bf_ref,
                 q_ref, k_ref, v_ref, kb_ref, vb_ref, lf_ref, c_ref, us_ref, up_ref,
                 carry_ref, *, tm, sub, with_cumsum):
    i = pl.program_id(1)

    @pl.when(i == 0)
    def _():
        carry_ref[...] = jnp.zeros_like(carry_ref)

    row = lax.broadcasted_iota(jnp.int32, (sub, sub), 0)
    col = lax.broadcasted_iota(jnp.int32, (sub, sub), 1)
    tri = (row >= col).astype(F32)
    carry = carry_ref[...]
    for r in range(tm // sub):
        rows = pl.ds(r * sub, sub)
        h = _rms(x_ref[rows, :], g_ref[...]).astype(BF16)
        qkv = _dot(h, wqkv_ref[...])
        q_ref[rows, :] = (qkv[:, :ATT_WIDTH] * (LOG2E * HEAD_DIM ** -0.5)).astype(BF16)
        k = qkv[:, ATT_WIDTH:2 * ATT_WIDTH]
        v = qkv[:, 2 * ATT_WIDTH:]
        k_ref[rows, :] = k
        v_ref[rows, :] = v
        kb_ref[rows, :] = k.astype(BF16)
        vb_ref[rows, :] = v.astype(BF16)
        f = _dot(h, wf_ref[...]) + bf_ref[...]
        lf = -(jnp.maximum(-f, 0.0) + jnp.log1p(jnp.exp(-jnp.abs(f))))
        lf_ref[rows, :] = lf[:, :ATT_HEADS]
        if with_cumsum:
            c = carry + jnp.dot(tri, lf, precision=HIGHEST, preferred_element_type=F32)
            carry = c[sub - 1:sub, :]
        else:
            c = lf
        c_ref[rows, :] = c[:, :ATT_HEADS] * LOG2E
        sp = _dot(h, wsp_ref[...])
        us_ref[rows, :] = sp[:, :S5_WIDTH]
        up_ref[rows, :] = sp[:, S5_WIDTH:]
    carry_ref[...] = carry


def _proj(x, g, wqkv, wf, wsp, bf, tm, with_cumsum):
    B, L, D = x.shape
    nt = L // tm
    row = lambda w: pl.BlockSpec((None, tm, w), lambda b, i: (b, i, 0))
    out_shape = (
        jax.ShapeDtypeStruct((B, L, ATT_WIDTH), BF16),
        jax.ShapeDtypeStruct((B, L, ATT_WIDTH), F32),
        jax.ShapeDtypeStruct((B, L, ATT_WIDTH), F32),
        jax.ShapeDtypeStruct((B, L, ATT_WIDTH), BF16),
        jax.ShapeDtypeStruct((B, L, ATT_WIDTH), BF16),
        jax.ShapeDtypeStruct((B, L, ATT_HEADS), F32),
        jax.ShapeDtypeStruct((B, L, ATT_HEADS), F32),
        jax.ShapeDtypeStruct((B, L, S5_WIDTH), F32),
        jax.ShapeDtypeStruct((B, L, POOL_WIDTH), F32),
    )
    return pl.pallas_call(
        functools.partial(_proj_kernel, tm=tm, sub=min(tm, 256), with_cumsum=with_cumsum),
        grid=(B, nt),
        in_specs=[row(D), _resident(g.shape), _resident(wqkv.shape), _resident(wf.shape),
                  _resident(wsp.shape), _resident(bf.shape)],
        out_specs=(row(ATT_WIDTH),) * 5 + (row(ATT_HEADS),) * 2 + (row(S5_WIDTH), row(POOL_WIDTH)),
        out_shape=out_shape,
        scratch_shapes=[pltpu.VMEM((1, LANES), F32)],
        compiler_params=_params("arbitrary", "arbitrary"),
        name="proj",
    )(x, g, wqkv, wf, wsp, bf)


def _attn_kernel(q_ref, k_ref, v_ref, c_ref, o_ref, qm_ref, s_ref, m_ref, acc_ref, *, tq, sub):
    qi = pl.program_id(2)
    q2 = q_ref[...]
    lo = lax.broadcasted_iota(jnp.int32, (tq, LANES), 1) < HEAD_DIM
    zero = jnp.zeros_like(q2)
    qm_ref[0] = jnp.where(lo, q2, zero)
    qm_ref[1] = jnp.where(lo, zero, q2)
    m_ref[...] = jnp.full_like(m_ref, NEG_INF)
    acc_ref[...] = jnp.zeros_like(acc_ref)

    def scores(e, j, slot):
        start = pl.multiple_of(j * tq, tq)
        s_ref[slot, e] = _dot_nt(qm_ref[e], k_ref[pl.ds(start, tq), :]) - c_ref[e:e + 1, pl.ds(start, tq)]

    def update(e, j, slot, masked):
        start = pl.multiple_of(j * tq, tq)
        vj = v_ref[pl.ds(start, tq), :]
        one = jnp.ones_like(vj)
        vje = jnp.where(lo, vj, one) if e == 0 else jnp.where(lo, one, vj)
        nsub = tq // sub if masked else 1
        rows_per = tq // nsub
        for r in range(nsub):
            rows = pl.ds(r * rows_per, rows_per)
            width = (r + 1) * sub if masked else tq
            s = s_ref[slot, e, rows, 0:width]
            if masked:
                row = lax.broadcasted_iota(jnp.int32, (rows_per, width), 0) + r * sub
                col = lax.broadcasted_iota(jnp.int32, (rows_per, width), 1)
                s = jnp.where(col <= row, s, NEG_INF)
            m_old = m_ref[e, rows, :]
            m_new = jnp.maximum(m_old, jnp.max(s, axis=1, keepdims=True))
            alpha = jnp.exp2(m_old - m_new)
            p = jnp.exp2(s - jnp.concatenate([m_new] * (width // LANES), axis=1))
            acc_ref[e, rows, :] = alpha * acc_ref[e, rows, :] + _dot(p.astype(BF16), vje[0:width, :])
            m_ref[e, rows, :] = m_new

    scores(0, 0, 0)
    scores(1, 0, 0)

    def step(j, slot):
        for e in range(2):
            scores(e, j + 1, 1 - slot)
            update(e, j, slot, False)

    def body(jj, carry):
        step(2 * jj, 0)
        step(2 * jj + 1, 1)
        return carry

    npair = jnp.right_shift(qi, 1)
    lax.fori_loop(0, npair, body, 0)

    @pl.when(qi == 2 * npair)
    def _():
        update(0, qi, 0, True)
        update(1, qi, 0, True)

    @pl.when(qi != 2 * npair)
    def _():
        step(qi - 1, 0)
        update(0, qi, 1, True)
        update(1, qi, 1, True)

    a0 = acc_ref[0]
    a1 = acc_ref[1]
    l0 = pltpu.roll(a0, HEAD_DIM, axis=1)
    l1 = pltpu.roll(a1, HEAD_DIM, axis=1)
    o_ref[...] = jnp.where(lo, a0 / l0, a1 / l1).astype(BF16)


def _attn(q, kb, vb, ct, tq):
    B, L, _ = q.shape
    nq = L // tq
    npair = ATT_HEADS // 2
    return pl.pallas_call(
        functools.partial(_attn_kernel, tq=tq, sub=min(tq, LANES)),
        grid=(B, npair, nq),
        in_specs=[
            pl.BlockSpec((None, tq, LANES), lambda b, hp, qi: (b, qi, hp)),
            pl.BlockSpec((None, L, LANES), lambda b, hp, qi: (b, 0, hp)),
            pl.BlockSpec((None, L, LANES), lambda b, hp, qi: (b, 0, hp)),
            pl.BlockSpec((None, None, 2, L), lambda b, hp, qi: (b, hp, 0, 0)),
        ],
        out_specs=pl.BlockSpec((None, tq, LANES), lambda b, hp, qi: (b, qi, hp)),
        out_shape=jax.ShapeDtypeStruct((B, L, ATT_WIDTH), BF16),
        scratch_shapes=[pltpu.VMEM((2, tq, LANES), BF16), pltpu.VMEM((2, 2, tq, tq), F32),
                        pltpu.VMEM((2, tq, LANES), F32), pltpu.VMEM((2, tq, LANES), F32)],
        compiler_params=_params("arbitrary", "arbitrary", "arbitrary"),
        name="attn_prompt",
    )(q, kb, vb, ct)


def _split3_dot(a, b):
    a1 = a.astype(BF16)
    r1 = a - a1.astype(F32)
    a2 = r1.astype(BF16)
    a3 = (r1 - a2.astype(F32)).astype(BF16)
    return _dot(a1, b) + _dot(a2, b) + _dot(a3, b)


def _attn_sample_kernel(pt_ref, q_ref, kn_ref, vn_ref, lfn_ref, u_ref, *rest, n_pages, page, t_new):
    del pt_ref
    k_refs = rest[:n_pages]
    v_refs = rest[n_pages:2 * n_pages]
    lf_refs = rest[2 * n_pages:3 * n_pages]
    o_ref = rest[3 * n_pages]
    nh = ATT_HEADS
    nrow = t_new * nh
    sub = lax.broadcasted_iota(jnp.int32, (nh, ATT_WIDTH), 0)
    lane = lax.broadcasted_iota(jnp.int32, (nh, ATT_WIDTH), 1)
    head_mask = jnp.right_shift(lane, _log2(HEAD_DIM)) == sub
    q = q_ref[...].astype(F32)
    qexp = jnp.concatenate(
        [jnp.where(head_mask, jnp.broadcast_to(q[t:t + 1, :], (nh, ATT_WIDTH)), 0.0) for t in range(t_new)],
        axis=0)
    qexp_b = qexp.astype(BF16)

    lf = jnp.concatenate([r[...] for r in lf_refs], axis=0)
    local = _split3_dot(lf, u_ref[...])
    carry = jnp.zeros((nh, 1), F32)
    cum = []
    for j in range(n_pages):
        cj = local[j * nh:(j + 1) * nh, :] + carry
        carry = cj[:, page - 1:page]
        cum.append(cj)
    cum = jnp.concatenate(cum, axis=0) * LOG2E

    s_pages = []
    for j in range(n_pages):
        kt = k_refs[j][...].reshape(ATT_WIDTH, page).astype(BF16)
        bj = jnp.concatenate([cum[j * nh:(j + 1) * nh, :]] * t_new, axis=0)
        s_pages.append(_dot(qexp_b, kt) - bj)

    kn = kn_ref[...].astype(BF16).astype(F32)
    vn = vn_ref[...].astype(BF16).astype(F32)
    lfn = lfn_ref[...]
    tq_of_row = jnp.right_shift(lax.broadcasted_iota(jnp.int32, (nrow, 1), 0), _log2(nh))
    s_new = []
    for t in range(t_new):
        carry = carry + lfn[:, t:t + 1]
        st = (jnp.sum(qexp * kn[t:t + 1, :], axis=-1, keepdims=True)
              - jnp.concatenate([carry * LOG2E] * t_new, axis=0))
        s_new.append(jnp.where(tq_of_row >= t, st, NEG_INF))

    m = s_new[0]
    for st in s_new[1:]:
        m = jnp.maximum(m, st)
    for s in s_pages:
        m = jnp.maximum(m, jnp.max(s, axis=-1, keepdims=True))
    l = jnp.zeros((nrow, 1), F32)
    acc = jnp.zeros((nrow, ATT_WIDTH), F32)
    for j in range(n_pages):
        p = jnp.exp2(s_pages[j] - m)
        l = l + jnp.sum(p, axis=-1, keepdims=True)
        vt = v_refs[j][...].reshape(ATT_WIDTH, page).astype(BF16)
        acc = acc + _dot_nt(p.astype(BF16), vt)
    for t in range(t_new):
        p = jnp.exp2(s_new[t] - m)
        l = l + p
        acc = acc + p.astype(BF16).astype(F32) * vn[t:t + 1, :]
    out = acc / l
    rows = [jnp.sum(jnp.where(head_mask, out[t * nh:(t + 1) * nh, :], 0.0), axis=0, keepdims=True)
            for t in range(t_new)]
    o_ref[...] = jnp.concatenate(rows, axis=0).astype(BF16)


def _attn_sample(page_table, q, kn, vn, lfn, upper, cache_kt, cache_vt, cache_lft, layer):
    Bd, T, _ = q.shape
    n_pages = page_table.shape[1]
    page = cache_lft.shape[3]

    def page_spec(shape, j):
        return pl.BlockSpec((None, None) + shape, lambda b, pt: (layer, pt[b, j]) + (0,) * len(shape))

    seq = lambda shape: pl.BlockSpec((None,) + shape, lambda b, pt: (b, 0, 0))
    const = lambda a: pl.BlockSpec(a.shape, lambda b, pt: (0, 0), pipeline_mode=pl.Buffered(1))
    in_specs = [seq((T, ATT_WIDTH))] * 3 + [seq((ATT_HEADS, LANES)), const(upper)]
    in_specs += [page_spec((ATT_HEADS, HEAD_DIM, page), j) for j in range(n_pages)]
    in_specs += [page_spec((ATT_HEADS, HEAD_DIM, page), j) for j in range(n_pages)]
    in_specs += [page_spec((ATT_HEADS, page), j) for j in range(n_pages)]
    grid_spec = pltpu.PrefetchScalarGridSpec(
        num_scalar_prefetch=1, grid=(Bd,), in_specs=in_specs,
        out_specs=pl.BlockSpec((None, T, ATT_WIDTH), lambda b, pt: (b, 0, 0)))
    return pl.pallas_call(
        functools.partial(_attn_sample_kernel, n_pages=n_pages, page=page, t_new=T),
        grid_spec=grid_spec,
        out_shape=jax.ShapeDtypeStruct((Bd, T, ATT_WIDTH), BF16),
        compiler_params=_params("arbitrary"),
        name="attn_sample",
    )(page_table, q, kn, vn, lfn, upper,
      *([cache_kt] * n_pages), *([cache_vt] * n_pages), *([cache_lft] * n_pages))


def _split3(x):
    x1 = x.astype(BF16)
    r1 = x - x1.astype(F32)
    x2 = r1.astype(BF16)
    return x1, x2, (r1 - x2.astype(F32)).astype(BF16)


def _s5_kernel(*refs, R, Tt, two_pass, perm):
    if perm:
        u_ref, pm_ref, pmt_ref, *refs = refs
    else:
        u_ref, *refs = refs
    (h0_ref, lre_ref, lim_ref, ldt_ref, bblk_ref, cblk_ref, d_ref, w1_ref, b1_ref, w2_ref, b2_ref,
     y_ref, hfin_ref, x_scr, h_scr, st_scr, end_scr, bz_scr) = refs
    p = pl.program_id(0)
    i = pl.program_id(1)
    nt = pl.num_programs(1)
    n = S5_N
    last_pass = 1 if two_pass else 0

    dt = jnp.exp(ldt_ref[...])
    lr = lre_ref[...]
    li = lim_ref[...]
    mag = jnp.exp(lr * dt)
    a_re = mag * jnp.cos(li * dt)
    a_im = mag * jnp.sin(li * dt)
    den = lr * lr + li * li
    z_re = ((a_re - 1.0) * lr + a_im * li) / den
    z_im = (a_im * lr - (a_re - 1.0) * li) / den

    @pl.when((i == 0) & (p == 0))
    def _():
        st_scr[...] = h0_ref[...]
        b_re = bblk_ref[:, :n]
        b_im = bblk_ref[:, n:]
        bz_scr[:, :n] = (z_re * b_re - z_im * b_im).astype(BF16)
        bz_scr[:, n:] = (z_re * b_im + z_im * b_re).astype(BF16)

    if perm:
        u_parts = _split3(u_ref[...].reshape(R * Tt, S5_WIDTH))
        u_b = _dot(pm_ref[...], u_parts[0]).astype(BF16)
    else:
        u_b = u_ref[...].astype(BF16)
    x_scr[...] = _dot(u_b, bz_scr[...])

    if two_pass:
        @pl.when((i == 0) & (p == 1))
        def _():
            rolled = pltpu.roll(end_scr[...], R // 2, axis=0)
            first = lax.broadcasted_iota(jnp.int32, (R, 2 * n), 0) < R // 2
            st_scr[...] = jnp.where(first, h0_ref[...], rolled)

    ar = jnp.broadcast_to(a_re, (R, n))
    ai = jnp.broadcast_to(a_im, (R, n))

    def scan(store):
        def step(t, carry):
            hr, hi = carry
            r0 = t * R if isinstance(t, int) else pl.multiple_of(t * R, R)
            xr = x_scr[pl.ds(r0, R), :n]
            xi = x_scr[pl.ds(r0, R), n:]
            nr = ar * hr - ai * hi + xr
            ni = ar * hi + ai * hr + xi
            if store:
                h_scr[pl.ds(r0, R), :n] = nr
                h_scr[pl.ds(r0, R), n:] = ni
            return nr, ni

        if R == SUBLANES:
            hr, hi = lax.fori_loop(0, Tt, step, (st_scr[:, :n], st_scr[:, n:]), unroll=2)
        else:
            hr, hi = st_scr[:, :n], st_scr[:, n:]
            for t in range(Tt):
                hr, hi = step(t, (hr, hi))
        st_scr[:, :n] = hr
        st_scr[:, n:] = hi

    if two_pass:
        @pl.when(p == 0)
        def _():
            scan(False)

            @pl.when(i == nt - 1)
            def _():
                end_scr[...] = st_scr[...]

    @pl.when(p == last_pass)
    def _():
        scan(True)
        if perm:
            u = u_b.astype(F32) + _dot(pm_ref[...], u_parts[1]) + _dot(pm_ref[...], u_parts[2])
        else:
            u = u_ref[...]
        y = _dot(h_scr[...].astype(BF16), cblk_ref[...]) + d_ref[...] * u
        z = _gelu(y).astype(BF16)
        out = ((_dot(z, w1_ref[...]) + b1_ref[...]) * _sigmoid(_dot(z, w2_ref[...]) + b2_ref[...])).astype(BF16)
        if perm:
            y_ref[...] = _dot(pmt_ref[...], out).astype(BF16).reshape(y_ref.shape)
        else:
            y_ref[...] = out

        @pl.when(i == nt - 1)
        def _():
            hfin_ref[...] = st_scr[...]


def _s5(u, h0, lre, lim, ldt, bblk, cblk, d, w1, b1, w2, b2, *, R, Tt, two_pass, perm=None):
    tile = R * Tt
    npass = 2 if two_pass else 1
    n2 = 2 * S5_N
    step = lambda p, i: i * p if two_pass else i
    if perm is None:
        nt = u.shape[0] // tile
        u_spec = pl.BlockSpec((tile, S5_WIDTH), lambda p, i: (i, 0))
        y_spec = pl.BlockSpec((tile, S5_WIDTH), lambda p, i: (step(p, i), 0))
        lead = ()
    else:
        nb, nh = u.shape[:2]
        nt = u.shape[2] // Tt
        u_spec = pl.BlockSpec((nb, nh, Tt, S5_WIDTH), lambda p, i: (0, 0, i, 0))
        y_spec = pl.BlockSpec((nb, nh, Tt, S5_WIDTH), lambda p, i: (0, 0, step(p, i), 0))
        lead = tuple(perm)
    consts = lead + (h0, lre, lim, ldt, bblk, cblk, d, w1, b1, w2, b2)
    return pl.pallas_call(
        functools.partial(_s5_kernel, R=R, Tt=Tt, two_pass=two_pass, perm=perm is not None),
        grid=(npass, nt),
        in_specs=[u_spec] + [_resident(a.shape) for a in consts],
        out_specs=(y_spec, pl.BlockSpec((R, n2), lambda p, i: (0, 0))),
        out_shape=(jax.ShapeDtypeStruct(u.shape, BF16), jax.ShapeDtypeStruct((R, n2), F32)),
        scratch_shapes=[pltpu.VMEM((tile, n2), F32), pltpu.VMEM((tile, n2), F32),
                        pltpu.VMEM((R, n2), F32), pltpu.VMEM((R, n2), F32),
                        pltpu.VMEM((S5_WIDTH, n2), BF16)],
        compiler_params=_params("arbitrary", "arbitrary"),
        name="s5",
    )(u, *consts)


def _pool_kernel(p_ref, halo_ref, w_ref, sc_ref, y_ref, ext, a2, a4, a8, a16, *, R, tm, start_pos, nt):
    i = pl.program_id(1)
    H = POOL_HALO * R
    n = H + tm

    @pl.when(i == 0)
    def _():
        ext[0:H, :] = halo_ref[...]

    x = p_ref[...]
    ext[H:n, :] = x
    n2, n4, n8, n16 = n - R, n - 3 * R, n - 7 * R, n - 15 * R
    a2[0:n2, :] = ext[R:R + n2, :] + ext[0:n2, :]
    a4[0:n4, :] = a2[2 * R:2 * R + n4, :] + a2[0:n4, :]
    a8[0:n8, :] = a4[4 * R:4 * R + n8, :] + a4[0:n8, :]
    a16[0:n16, :] = a8[8 * R:8 * R + n16, :] + a8[0:n16, :]
    s2 = a2[15 * R:15 * R + tm, :]
    s4 = a4[13 * R:13 * R + tm, :]
    s8 = a8[9 * R:9 * R + tm, :]
    s16 = a16[R:R + tm, :]
    pos = start_pos + jnp.right_shift(i * tm + lax.broadcasted_iota(jnp.int32, (tm, POOL_WIDTH), 0), _log2(R))
    lane = lax.broadcasted_iota(jnp.int32, (tm, POOL_WIDTH), 1)
    cnt = lambda w: jnp.minimum(w, pos + 1).astype(F32)
    mean = jnp.where(lane < POOL_GW, s2 / cnt(2),
                     jnp.where(lane < 2 * POOL_GW, s4 / cnt(4),
                               jnp.where(lane < 3 * POOL_GW, s8 / cnt(8), s16 / cnt(16))))
    m = (mean - x).astype(BF16)
    y_ref[...] = (_dot(m, w_ref[...]) * sc_ref[...]).astype(BF16)
    if nt > 1:
        ext[0:H, :] = ext[tm:tm + H, :]


def _pool(p, halo, wblk, scale, *, R, tm, start_pos):
    G, L, W = p.shape
    nt = L // tm
    H = POOL_HALO * R
    buf = pltpu.VMEM((H + tm, W), F32)
    return pl.pallas_call(
        functools.partial(_pool_kernel, R=R, tm=tm, start_pos=start_pos, nt=nt),
        grid=(G, nt),
        in_specs=[pl.BlockSpec((None, tm, W), lambda g, i: (g, i, 0)),
                  pl.BlockSpec((None, H, W), lambda g, i: (g, 0, 0)),
                  _resident(wblk.shape), _resident(scale.shape)],
        out_specs=pl.BlockSpec((None, tm, W), lambda g, i: (g, i, 0)),
        out_shape=jax.ShapeDtypeStruct((G, L, W), BF16),
        scratch_shapes=[buf] * 5,
        compiler_params=_params("arbitrary", "arbitrary"),
        name="pool",
    )(p, halo, wblk, scale)


def _merge_kernel(x_ref, ya_ref, ys_ref, yp_ref, g1_ref, wg_ref, wa_ref, wb_ref, wc_ref, wo_ref, g2_ref,
                  x1_ref, h2_ref, *, sub):
    d = x_ref.shape[-1]
    for r in range(x_ref.shape[0] // sub):
        rows = pl.ds(r * sub, sub)
        x = x_ref[rows, :]
        h = _rms(x, g1_ref[...]).astype(BF16)
        merged = (_sigmoid(_dot(h, wg_ref[:, :d])) * _dot(ya_ref[rows, :], wa_ref[...])
                  + _sigmoid(_dot(h, wg_ref[:, d:2 * d])) * _dot(ys_ref[rows, :], wb_ref[...])
                  + _sigmoid(_dot(h, wg_ref[:, 2 * d:])) * _dot(yp_ref[rows, :], wc_ref[...]))
        x1 = x + _dot(merged.astype(BF16), wo_ref[...])
        x1_ref[rows, :] = x1
        h2_ref[rows, :] = _rms(x1, g2_ref[...]).astype(BF16)


def _merge(x, ya, ys, yp, g1, wg, wa, wb, wc, wo, g2, tm):
    N, D = x.shape
    row = lambda w: pl.BlockSpec((tm, w), lambda i: (i, 0))
    consts = (g1, wg, wa, wb, wc, wo, g2)
    return pl.pallas_call(
        functools.partial(_merge_kernel, sub=min(tm, 256)),
        grid=(N // tm,),
        in_specs=[row(D), row(ATT_WIDTH), row(S5_WIDTH), row(POOL_WIDTH)] + [_resident(a.shape) for a in consts],
        out_specs=(row(D), row(D)),
        out_shape=(jax.ShapeDtypeStruct((N, D), F32), jax.ShapeDtypeStruct((N, D), BF16)),
        compiler_params=_params("arbitrary"),
        name="merge",
    )(x, ya, ys, yp, *consts)


def _ffn_kernel(h2_ref, x1_ref, halo_ref, wup_ref, cw_ref, cb_ref, wdn_ref, gfin_ref,
                x2_ref, tail_ref, ext, tail, act, *, R, tm, sub, H, fc, final_norm):
    i = pl.program_id(1)
    nt = pl.num_programs(1)
    dff = wdn_ref.shape[0]
    nc = dff // fc
    ns = tm // sub

    @pl.when(i == 0)
    def _():
        tail[...] = halo_ref[...]

    offsets = lambda c: (c * fc, dff + c * fc)

    def up(s, c):
        h2 = h2_ref[pl.ds(s * sub, sub), :]
        return [_dot(h2, wup_ref[:, off:off + fc]) for off in offsets(c)]

    def gate(s, c, us):
        halves = []
        for k, off in enumerate(offsets(c)):
            u = us[k]
            ext[k, 0:H, :] = tail[:, off:off + fc]
            ext[k, H:H + sub, :] = u
            cw = cw_ref[:, off:off + fc]
            cv = (cb_ref[:, off:off + fc] + cw[0:1, :] * ext[k, H - 2 * R:H - 2 * R + sub, :]
                  + cw[1:2, :] * ext[k, H - R:H - R + sub, :] + cw[2:3, :] * u)
            tail[:, off:off + fc] = ext[k, sub:sub + H, :]
            halves.append(cv)
        act[s, :, c * fc:(c + 1) * fc] = (_gelu(halves[0]) * halves[1]).astype(BF16)

    def down(s):
        rows = pl.ds(s * sub, sub)
        out = x1_ref[rows, :] + _dot(act[s], wdn_ref[...])
        if final_norm:
            out = _rms(out, gfin_ref[...])
        x2_ref[rows, :] = out

    cur = up(0, 0)
    for s in range(ns):
        for c in range(nc):
            if c + 1 < nc:
                nxt = up(s, c + 1)
            else:
                nxt = up(s + 1, 0) if s + 1 < ns else None
            gate(s, c, cur)
            cur = nxt
        down(s)

    @pl.when(i == nt - 1)
    def _():
        tail_ref[...] = tail[...]


def _ffn(h2, x1, halo, wup, cw, cb, wdn, gfin, *, R, tm, final_norm):
    G, L, D = x1.shape
    F2 = wup.shape[1]
    H = halo.shape[1]
    fc = 256
    sub = max(H, min(tm, 256))
    dff = wdn.shape[0]
    consts = (wup, cw, cb, wdn, gfin)
    return pl.pallas_call(
        functools.partial(_ffn_kernel, R=R, tm=tm, sub=sub, H=H, fc=fc, final_norm=final_norm),
        grid=(G, L // tm),
        in_specs=[pl.BlockSpec((None, tm, D), lambda g, i: (g, i, 0)),
                  pl.BlockSpec((None, tm, D), lambda g, i: (g, i, 0)),
                  pl.BlockSpec((None, H, F2), lambda g, i: (g, 0, 0))] + [_resident(a.shape) for a in consts],
        out_specs=(pl.BlockSpec((None, tm, D), lambda g, i: (g, i, 0)),
                   pl.BlockSpec((None, H, F2), lambda g, i: (g, 0, 0))),
        out_shape=(jax.ShapeDtypeStruct((G, L, D), F32), jax.ShapeDtypeStruct((G, H, F2), F32)),
        scratch_shapes=[pltpu.VMEM((2, H + sub, fc), F32), pltpu.VMEM((H, F2), F32),
                        pltpu.VMEM((tm // sub, sub, dff), BF16)],
        compiler_params=_params("arbitrary", "arbitrary"),
        name="ffn",
    )(h2, x1, halo, *consts)


def _blockdiag(w):
    G, a, b = w.shape
    eye = jnp.eye(G, dtype=w.dtype)
    return (w[:, :, None, :] * eye[:, None, :, None]).reshape(G * a, G * b)


def _layer_weights(l, w):
    d = w["w_in"].shape[1]
    o_f = 3 * ATT_WIDTH
    o_s = o_f + ATT_HEADS
    o_g = o_s + S5_WIDTH + POOL_WIDTH
    w_in = w["w_in"][l]
    row = lambda a: a.reshape(1, -1).astype(F32)
    bblk = jnp.concatenate([_blockdiag(jnp.swapaxes(w["s5_b_re"][l], 1, 2)),
                            _blockdiag(jnp.swapaxes(w["s5_b_im"][l], 1, 2))], axis=1)
    cblk = jnp.concatenate([_blockdiag(jnp.swapaxes(w["s5_c_re"][l], 1, 2)),
                            -_blockdiag(jnp.swapaxes(w["s5_c_im"][l], 1, 2))], axis=0)
    return dict(
        g1=row(w["norm_mix_g"][l]),
        wqkv=w_in[:, :o_f].astype(BF16),
        wf=jnp.pad(w_in[:, o_f:o_s], ((0, 0), (0, LANES - ATT_HEADS))).astype(BF16),
        wsp=w_in[:, o_s:o_g].astype(BF16),
        wg=w_in[:, o_g:].astype(BF16),
        bf=jnp.pad(row(w["b_f"][l]), ((0, 0), (0, LANES - ATT_HEADS))),
        lre=row(w["s5_lam_re"][l]), lim=row(w["s5_lam_im"][l]),
        ldt=row(jnp.broadcast_to(w["s5_log_dt"][l][:, None], (S5_GROUPS, S5_STATE))),
        bblk=bblk.astype(F32), cblk=cblk.astype(BF16), d=row(w["s5_d"][l]),
        w1=w["s5_glu_w1"][l].astype(BF16), b1=row(w["s5_glu_b1"][l]),
        w2=w["s5_glu_w2"][l].astype(BF16), b2=row(w["s5_glu_b2"][l]),
        pw=_blockdiag(w["pool_w"][l]).astype(BF16), ps=row(w["pool_scale"][l]),
        wa=w["w_br_a"][l].astype(BF16), wb=w["w_br_b"][l].astype(BF16), wc=w["w_br_c"][l].astype(BF16),
        wo=w["w_out"][l].astype(BF16), g2=row(w["norm_ffn_g"][l]),
        wup=w["w_up"][l].astype(BF16), cw=w["conv_w"][l].astype(F32), cb=row(w["conv_b"][l]),
        wdn=w["w_down"][l].astype(BF16),
    )


def _tile(n, pref):
    return pref if n % pref == 0 else n


def kernel(x_prompt, x_sample, cache_k, cache_v, cache_logf, state_ssm_re, state_ssm_im, state_pool,
           state_ffn_conv, page_table, norm_mix_g, w_in, b_f, s5_lam_re, s5_lam_im, s5_log_dt, s5_b_re, s5_b_im,
           s5_c_re, s5_c_im, s5_d, s5_glu_w1, s5_glu_b1, s5_glu_w2, s5_glu_b2, pool_w, pool_scale, w_br_a,
           w_br_b, w_br_c, w_out, norm_ffn_g, w_up, conv_w, conv_b, w_down, norm_final_g):
    w = dict(norm_mix_g=norm_mix_g, w_in=w_in, b_f=b_f, s5_lam_re=s5_lam_re, s5_lam_im=s5_lam_im,
             s5_log_dt=s5_log_dt, s5_b_re=s5_b_re, s5_b_im=s5_b_im, s5_c_re=s5_c_re, s5_c_im=s5_c_im, s5_d=s5_d,
             s5_glu_w1=s5_glu_w1, s5_glu_b1=s5_glu_b1, s5_glu_w2=s5_glu_w2, s5_glu_b2=s5_glu_b2, pool_w=pool_w,
             pool_scale=pool_scale, w_br_a=w_br_a, w_br_b=w_br_b, w_br_c=w_br_c, w_out=w_out,
             norm_ffn_g=norm_ffn_g, w_up=w_up, conv_w=conv_w, conv_b=conv_b, w_down=w_down)
    depth = w_in.shape[0]
    B, L, D = x_prompt.shape
    Bd, T, _ = x_sample.shape
    F2 = w_up.shape[2]
    n_pool, page = cache_k.shape[1], cache_k.shape[2]
    n2 = 2 * S5_N
    gfin = norm_final_g.reshape(1, D).astype(F32)

    tm = _tile(L, 512)
    tr = _tile(L, 1024)
    halves = 2
    Lh = L // halves
    Rp = halves * B
    assert Rp == SUBLANES, "prompt S5 scan keeps one (half, batch) group per vreg"
    Tt = _tile(Lh, 64)
    Ns = Bd * T

    ck = jnp.transpose(cache_k, (0, 1, 3, 4, 2))
    cv = jnp.transpose(cache_v, (0, 1, 3, 4, 2))
    clft = jnp.swapaxes(cache_logf, 2, 3)
    pos_r = lax.broadcasted_iota(jnp.int32, (page, page), 0)
    pos_c = lax.broadcasted_iota(jnp.int32, (page, page), 1)
    upper = (pos_r <= pos_c).astype(BF16)
    pool_halo_s = jnp.pad(jnp.swapaxes(state_pool, 1, 2), ((0, 0), (1, 0), (0, 0), (0, 0)))
    pool_halo_s = pool_halo_s.reshape(depth, 1, POOL_HALO * Bd, POOL_WIDTH)
    conv_halo_s = jnp.swapaxes(state_ffn_conv, 1, 2).reshape(depth, 1, (CONV_TAPS - 1) * Bd, F2)
    h0_s = jnp.concatenate([state_ssm_re.reshape(depth, Bd, S5_N), state_ssm_im.reshape(depth, Bd, S5_N)], axis=-1)
    h0_p = jnp.zeros((Rp, n2), F32)
    dst = lax.broadcasted_iota(jnp.int32, (Rp * Tt, Rp * Tt), 0)
    src = lax.broadcasted_iota(jnp.int32, (Rp * Tt, Rp * Tt), 1)
    seg, t_in = dst % Rp, dst // Rp
    pm = (src == ((seg % B) * halves + seg // B) * Tt + t_in).astype(BF16)
    s5_perm = (pm, pm.T)
    pool_halo_p = jnp.zeros((B, POOL_HALO, POOL_WIDTH), F32)
    conv_halo_p = jnp.zeros((B, SUBLANES, F2), F32)

    xp = x_prompt
    xs = jnp.swapaxes(x_sample, 0, 1).reshape(1, Ns, D)
    st_p, st_s = [], []
    for l in range(depth):
        lw = _layer_weights(l, w)
        last = l == depth - 1

        q, k, v, kb, vb, lf, c, us, up = _proj(xp, lw["g1"], lw["wqkv"], lw["wf"], lw["wsp"], lw["bf"], tr, True)
        ct = jnp.swapaxes(c, 1, 2).reshape(B, ATT_HEADS // 2, 2, L)
        ya = _attn(q, kb, vb, ct, tm)
        ys, hfin = _s5(us.reshape(B, halves, Lh, S5_WIDTH), h0_p, lw["lre"], lw["lim"], lw["ldt"], lw["bblk"],
                       lw["cblk"], lw["d"], lw["w1"], lw["b1"], lw["w2"], lw["b2"], R=Rp, Tt=Tt, two_pass=True,
                       perm=s5_perm)
        ys = ys.reshape(B * L, S5_WIDTH)
        yp = _pool(up, pool_halo_p, lw["pw"], lw["ps"], R=1, tm=tr, start_pos=0)
        x1, h2 = _merge(xp.reshape(B * L, D), ya.reshape(B * L, ATT_WIDTH), ys, yp.reshape(B * L, POOL_WIDTH),
                        lw["g1"], lw["wg"], lw["wa"], lw["wb"], lw["wc"], lw["wo"], lw["g2"], tr)
        xp, tail_p = _ffn(h2.reshape(B, L, D), x1.reshape(B, L, D), conv_halo_p, lw["wup"], lw["cw"], lw["cb"],
                          lw["wdn"], gfin, R=1, tm=tr, final_norm=last)
        st_p.append((k.reshape(B, L, ATT_HEADS, HEAD_DIM), v.reshape(B, L, ATT_HEADS, HEAD_DIM), lf,
                     hfin[B:, :S5_N].reshape(B, S5_GROUPS, S5_STATE), hfin[B:, S5_N:].reshape(B, S5_GROUPS, S5_STATE),
                     up[:, L - POOL_BUF:, :], tail_p[:, SUBLANES - (CONV_TAPS - 1):, :]))

        q, k, v, _, _, lf, _, us, up = _proj(xs, lw["g1"], lw["wqkv"], lw["wf"], lw["wsp"], lw["bf"], Ns, False)
        bt = lambda a: jnp.swapaxes(a.reshape(T, Bd, -1), 0, 1)
        k_b, v_b, lf_b = bt(k), bt(v), bt(lf)
        lfn = jnp.pad(jnp.swapaxes(lf_b, 1, 2), ((0, 0), (0, 0), (0, LANES - T)))
        ya = _attn_sample(page_table, bt(q), k_b, v_b, lfn, upper, ck, cv, clft, l)
        ya = jnp.swapaxes(ya, 0, 1).reshape(Ns, ATT_WIDTH)
        ys, hfin = _s5(us.reshape(Ns, S5_WIDTH), h0_s[l], lw["lre"], lw["lim"], lw["ldt"], lw["bblk"], lw["cblk"],
                       lw["d"], lw["w1"], lw["b1"], lw["w2"], lw["b2"], R=Bd, Tt=T, two_pass=False)
        yp = _pool(up, pool_halo_s[l], lw["pw"], lw["ps"], R=Bd, tm=Ns, start_pos=page_table.shape[1] * page)
        x1, h2 = _merge(xs.reshape(Ns, D), ya, ys, yp.reshape(Ns, POOL_WIDTH),
                        lw["g1"], lw["wg"], lw["wa"], lw["wb"], lw["wc"], lw["wo"], lw["g2"], Ns)
        xs, tail_s = _ffn(h2.reshape(1, Ns, D), x1.reshape(1, Ns, D), conv_halo_s[l], lw["wup"], lw["cw"], lw["cb"],
                          lw["wdn"], gfin, R=Bd, tm=Ns, final_norm=last)
        new_pool = jnp.concatenate([state_pool[l][:, T:, :], bt(up)], axis=1)
        new_conv = jnp.swapaxes(tail_s.reshape(CONV_TAPS - 1, Bd, F2), 0, 1)
        st_s.append((k_b.reshape(Bd, T, ATT_HEADS, HEAD_DIM), v_b.reshape(Bd, T, ATT_HEADS, HEAD_DIM), lf_b,
                     hfin[:, :S5_N].reshape(Bd, S5_GROUPS, S5_STATE), hfin[:, S5_N:].reshape(Bd, S5_GROUPS, S5_STATE),
                     new_pool, new_conv))

    y_prompt = xp
    y_sample = jnp.swapaxes(xs.reshape(T, Bd, D), 0, 1)
    outs_p = [jnp.stack(z) for z in zip(*st_p)]
    outs_s = [jnp.stack(z) for z in zip(*st_s)]
    return (y_prompt, y_sample, *outs_p, *outs_s)
```
